```python
import math
import jax, jax.numpy as jnp
from jax import lax
import numpy as np

D_MODEL = 1024
BATCH = 32
SEQ = 2048
DEPTH = 2
DEC_BATCH = 8
DEC_SEQ = 2048
PAST_LEN = 128

GRID_W = 64
HEAD_DIM = 64
QBLK = 128
ROPE_THETA = 10000.0
EPS = 1e-6
N_BRANCH = 4
A_HEADS = 4
A_D = HEAD_DIM // 2
A_VD = HEAD_DIM
B_HEADS = 4
B_D = HEAD_DIM
WIN_R = 8
WIN_C = 16
B_QCB = WIN_C
B_KCB = 2 * WIN_C
C_HEADS = 4
C_NOPE = 64
C_ROPE = 32
C_VD = 64
C_QLORA = 192
C_KVLORA = 128
D_HEADS = 4
D_KV_HEADS = 2
D_D = HEAD_DIM
D_GROUP = D_HEADS // D_KV_HEADS
BRANCH_W = 256
A_QK_COLS = A_HEADS * 2 * A_D
A_COLS = 2 * A_QK_COLS + A_HEADS * A_VD
B_COLS = 3 * B_HEADS * B_D
C_COLS = C_QLORA + C_KVLORA + C_ROPE
D_COLS = (D_HEADS + 2 * D_KV_HEADS) * D_D
GATE_COLS = N_BRANCH * D_MODEL
IN_COLS = A_COLS + B_COLS + C_COLS + D_COLS + GATE_COLS
FFN_DIM = 2816
CONV_W = 3

kernel_name = "hybrid_gated_parallel_encoder"


def rmsnorm(x, g):
    x32 = x.astype(jnp.float32)
    y = x32 * lax.rsqrt(jnp.mean(x32 * x32, axis=-1, keepdims=True) + EPS)
    return (y * g.astype(jnp.float32)).astype(x.dtype)


def rope_angles(pos, dim):
    inv_freq = ROPE_THETA ** (-jnp.arange(0, dim, 2, dtype=jnp.float32) / dim)
    return pos.astype(jnp.float32)[:, None] * inv_freq[None, :]


def apply_rope(x, ang):
    half = x.shape[-1] // 2
    shape = (1, ang.shape[0]) + (1,) * (x.ndim - 3) + (half,)
    cos = jnp.cos(ang).reshape(shape).astype(x.dtype)
    sin = jnp.sin(ang).reshape(shape).astype(x.dtype)
    x1, x2 = x[..., :half], x[..., half:]
    return jnp.concatenate([x1 * cos - x2 * sin, x2 * cos + x1 * sin], axis=-1)


def axial_rope(x, ang_row, ang_col):
    half = x.shape[-1] // 2
    return jnp.concatenate([apply_rope(x[..., :half], ang_row), apply_rope(x[..., half:], ang_col)], axis=-1)


def to_query_blocks(a):
    b, s = a.shape[:2]
    a = a.reshape((b, s // QBLK, QBLK) + a.shape[2:])
    return jnp.moveaxis(a, 1, 0)


def from_query_blocks(o):
    o = jnp.moveaxis(o, 0, 1)
    return o.reshape((o.shape[0], o.shape[1] * o.shape[2]) + o.shape[3:])


def diff_mixer(cols, ang, lq1, lk1, lq2, lk2, subln, lam_init):
    b, s, _ = cols.shape
    f32 = jnp.float32
    q, k, v = jnp.split(cols, [A_QK_COLS, 2 * A_QK_COLS], axis=-1)
    q = apply_rope(q.reshape(b, s, A_HEADS, 2, A_D), ang) * (A_D ** -0.5)
    k = apply_rope(k.reshape(b, s, A_HEADS, 2, A_D), ang)
    v = v.reshape(b, s, A_HEADS, A_VD)
    lam = (jnp.exp(jnp.sum(lq1.astype(f32) * lk1.astype(f32)))
           - jnp.exp(jnp.sum(lq2.astype(f32) * lk2.astype(f32))) + lam_init)

    def block(qb):
        sc = jnp.einsum('bqhcd,bkhcd->bhcqk', qb, k).astype(f32)
        p = jax.nn.softmax(sc, axis=-1)
        w = (p[:, :, 0] - lam * p[:, :, 1]).astype(v.dtype)
        return jnp.einsum('bhqk,bkhd->bqhd', w, v)

    o = from_query_blocks(lax.map(block, to_query_blocks(q)))
    o = rmsnorm(o, subln) * (1.0 - lam_init)
    return o.reshape(b, s, A_HEADS * A_VD)


def natten_mixer(cols, rpb):
    b, s, _ = cols.shape
    hd = B_HEADS * B_D
    q, k, v = jnp.split(cols, [hd, 2 * hd], axis=-1)
    rows = s // GRID_W
    wr = min(WIN_R, rows)
    ncb = GRID_W // B_QCB
    qg = (q * (B_D ** -0.5)).reshape(b, rows, ncb, B_QCB, B_HEADS, B_D)
    kg = k.reshape(b, rows, GRID_W, B_HEADS, B_D)
    vg = v.reshape(b, rows, GRID_W, B_HEADS, B_D)
    cb = np.clip(np.arange(ncb) * B_QCB - WIN_C // 2, 0, GRID_W - B_KCB)
    col_idx = cb[:, None] + np.arange(B_KCB)
    qcol = np.arange(GRID_W).reshape(ncb, B_QCB)
    cs = np.clip(qcol - WIN_C // 2, 0, GRID_W - WIN_C)
    kc = col_idx[:, None, :]
    col_mask = (kc >= cs[..., None]) & (kc < cs[..., None] + WIN_C)
    dc_idx = np.clip(kc - qcol[..., None] + WIN_C - 1, 0, 2 * WIN_C - 2)
    n_keys = wr * B_KCB
    mask = np.broadcast_to(col_mask[:, :, None, :], (ncb, B_QCB, wr, B_KCB)).reshape(ncb, B_QCB, n_keys)
    mask = jnp.asarray(mask)
    dc_b = jnp.asarray(dc_idx[:, :, None, :])

    def row_block(r):
        rs = jnp.clip(r - wr // 2, 0, rows - wr)
        k_blk = lax.dynamic_slice_in_dim(kg, rs, wr, axis=1)[:, :, col_idx]
        v_blk = lax.dynamic_slice_in_dim(vg, rs, wr, axis=1)[:, :, col_idx]
        k_blk = jnp.moveaxis(k_blk, 1, 2).reshape(b, ncb, n_keys, B_HEADS, B_D)
        v_blk = jnp.moveaxis(v_blk, 1, 2).reshape(b, ncb, n_keys, B_HEADS, B_D)
        q_blk = lax.dynamic_index_in_dim(qg, r, axis=1, keepdims=False)
        sc = jnp.einsum('bnqhd,bnlhd->bhnql', q_blk, k_blk).astype(jnp.float32)
        dr_idx = rs + jnp.arange(wr) - r + WIN_R - 1
        bias = rpb[:, dr_idx[None, None, :, None], dc_b]
        bias = bias.reshape(B_HEADS, ncb, B_QCB, n_keys).astype(jnp.float32)
        p = jax.nn.softmax(jnp.where(mask, sc + bias, -jnp.inf), axis=-1)
        o = jnp.einsum('bhnql,bnlhd->bnqhd', p.astype(v_blk.dtype), v_blk)
        return o.reshape(b, GRID_W, B_HEADS, B_D)

    out = lax.map(row_block, jnp.arange(rows))
    return jnp.moveaxis(out, 0, 1).reshape(b, s, hd)


def mla_mixer(cols, ang, q_norm, kv_norm, w_uq, w_ukv):
    b, s, _ = cols.shape
    cq, ckv, k_rope = jnp.split(cols, [C_QLORA, C_QLORA + C_KVLORA], axis=-1)
    q = (rmsnorm(cq, q_norm) @ w_uq).reshape(b, s, C_HEADS, C_NOPE + C_ROPE)
    q = jnp.concatenate([q[..., :C_NOPE], apply_rope(q[..., C_NOPE:], ang)], axis=-1)
    q = q * ((C_NOPE + C_ROPE) ** -0.5)
    kv = (rmsnorm(ckv, kv_norm) @ w_ukv).reshape(b, s, C_HEADS, C_NOPE + C_VD)
    k_nope, v = kv[..., :C_NOPE], kv[..., C_NOPE:]
    k_rope = apply_rope(k_rope, ang)[:, :, None, :]
    k = jnp.concatenate([k_nope, jnp.broadcast_to(k_rope, (b, s, C_HEADS, C_ROPE))], axis=-1)

    def block(qb):
        sc = jnp.einsum('bqhd,bkhd->bhqk', qb, k).astype(jnp.float32)
        p = jax.nn.softmax(sc, axis=-1).astype(v.dtype)
        return jnp.einsum('bhqk,bkhd->bqhd', p, v)

    o = from_query_blocks(lax.map(block, to_query_blocks(q)))
    return o.reshape(b, s, C_HEADS * C_VD)


def gqa_mixer(cols, ang_row, ang_col, q_norm, k_norm):
    b, s, _ = cols.shape
    qd = D_HEADS * D_D
    kvd = D_KV_HEADS * D_D
    q, k, v = jnp.split(cols, [qd, qd + kvd], axis=-1)
    q = axial_rope(rmsnorm(q.reshape(b, s, D_KV_HEADS, D_GROUP, D_D), q_norm), ang_row, ang_col) * (D_D ** -0.5)
    k = axial_rope(rmsnorm(k.reshape(b, s, D_KV_HEADS, D_D), k_norm), ang_row, ang_col)
    v = v.reshape(b, s, D_KV_HEADS, D_D)

    def block(qb):
        sc = jnp.einsum('bqngd,bknd->bngqk', qb, k).astype(jnp.float32)
        p = jax.nn.softmax(sc, axis=-1).astype(v.dtype)
        return jnp.einsum('bngqk,bknd->bqngd', p, v)

    o = from_query_blocks(lax.map(block, to_query_blocks(q)))
    return o.reshape(b, s, qd)


def conv_ffn(h, w_in, conv_w, conv_b, w_out):
    ug = h @ w_in
    u, g = ug[..., :FFN_DIM], ug[..., FFN_DIM:]
    gp = jnp.pad(g, ((0, 0), (1, 1), (0, 0)))
    g = gp[:, :-2] * conv_w[0] + gp[:, 1:-1] * conv_w[1] + gp[:, 2:] * conv_w[2] + conv_b
    return (jax.nn.silu(g) * u) @ w_out


def encoder_trunk(x, attn_norm, w_in, a_lambda_q1, a_lambda_k1, a_lambda_q2, a_lambda_k2, a_subln,
                  b_rpb, c_q_norm, c_kv_norm, c_w_uq, c_w_ukv, d_q_norm, d_k_norm,
                  w_branch, w_out, ffn_norm, w_ffn_in, ffn_conv_w, ffn_conv_b, w_ffn_out, final_norm):
    s = x.shape[1]
    t = jnp.arange(s)
    ang_a = rope_angles(t, A_D)
    ang_c = rope_angles(t, C_ROPE)
    ang_row = rope_angles(t // GRID_W, D_D // 2)
    ang_col = rope_angles(t % GRID_W, D_D // 2)
    p1 = A_COLS
    p2 = p1 + B_COLS
    p3 = p2 + C_COLS
    p4 = p3 + D_COLS
    for l in range(DEPTH):
        lam_init = 0.8 - 0.6 * math.exp(-0.3 * l)
        h = rmsnorm(x, attn_norm[l])
        proj = h @ w_in[l]
        a_cols, b_cols, c_cols, d_cols, g_cols = jnp.split(proj, [p1, p2, p3, p4], axis=-1)
        o_a = diff_mixer(a_cols, ang_a, a_lambda_q1[l], a_lambda_k1[l], a_lambda_q2[l], a_lambda_k2[l],
                         a_subln[l], lam_init)
        o_b = natten_mixer(b_cols, b_rpb[l])
        o_c = mla_mixer(c_cols, ang_c, c_q_norm[l], c_kv_norm[l], c_w_uq[l], c_w_ukv[l])
        o_d = gqa_mixer(d_cols, ang_row, ang_col, d_q_norm[l], d_k_norm[l])
        gates = jax.nn.sigmoid(g_cols)
        merged = None
        for i, o in enumerate((o_a, o_b, o_c, o_d)):
            term = gates[..., i * D_MODEL:(i + 1) * D_MODEL] * (o @ w_branch[l, i])
            merged = term if merged is None else merged + term
        x = x + merged @ w_out[l]
        x = x + conv_ffn(rmsnorm(x, ffn_norm[l]), w_ffn_in[l], ffn_conv_w[l], ffn_conv_b[l], w_ffn_out[l])
    return rmsnorm(x, final_norm)


def setup_inputs(seed: int = 0) -> dict:
    key = jax.random.key(seed)
    ks = jax.random.split(key, 24)
    f32 = jnp.float32

    def nrm(k, shape, scale):
        return jax.random.normal(k, shape, f32) * scale

    def gain(k, shape):
        return 1.0 + 0.02 * jax.random.normal(k, shape, f32)

    return {
        "x_prompt": nrm(ks[0], (BATCH, SEQ, D_MODEL), 1.0),
        "x_sample": nrm(ks[1], (DEC_BATCH, DEC_SEQ, D_MODEL), 1.0),
        "attn_norm": gain(ks[2], (DEPTH, D_MODEL)),
        "w_in": nrm(ks[3], (DEPTH, D_MODEL, IN_COLS), D_MODEL ** -0.5),
        "a_lambda_q1": nrm(ks[4], (DEPTH, A_D), 0.1),
        "a_lambda_k1": nrm(ks[5], (DEPTH, A_D), 0.1),
        "a_lambda_q2": nrm(ks[6], (DEPTH, A_D), 0.1),
        "a_lambda_k2": nrm(ks[7], (DEPTH, A_D), 0.1),
        "a_subln": gain(ks[8], (DEPTH, A_VD)),
        "b_rpb": nrm(ks[9], (DEPTH, B_HEADS, 2 * WIN_R - 1, 2 * WIN_C - 1), 0.02),
        "c_q_norm": gain(ks[10], (DEPTH, C_QLORA)),
        "c_kv_norm": gain(ks[11], (DEPTH, C_KVLORA)),
        "c_w_uq": nrm(ks[12], (DEPTH, C_QLORA, C_HEADS * (C_NOPE + C_ROPE)), C_QLORA ** -0.5),
        "c_w_ukv": nrm(ks[13], (DEPTH, C_KVLORA, C_HEADS * (C_NOPE + C_VD)), C_KVLORA ** -0.5),
        "d_q_norm": gain(ks[14], (DEPTH, D_D)),
        "d_k_norm": gain(ks[15], (DEPTH, D_D)),
        "w_branch": nrm(ks[16], (DEPTH, N_BRANCH, BRANCH_W, D_MODEL), BRANCH_W ** -0.5),
        "w_out": nrm(ks[17], (DEPTH, D_MODEL, D_MODEL), D_MODEL ** -0.5),
        "ffn_norm": gain(ks[18], (DEPTH, D_MODEL)),
        "w_ffn_in": nrm(ks[19], (DEPTH, D_MODEL, 2 * FFN_DIM), D_MODEL ** -0.5),
        "ffn_conv_w": nrm(ks[20], (DEPTH, CONV_W, FFN_DIM), CONV_W ** -0.5),
        "ffn_conv_b": nrm(ks[21], (DEPTH, FFN_DIM), 0.01),
        "w_ffn_out": nrm(ks[22], (DEPTH, FFN_DIM, D_MODEL), FFN_DIM ** -0.5),
        "final_norm": gain(ks[23], (D_MODEL,)),
    }


def reference(x_prompt, x_sample, attn_norm, w_in, a_lambda_q1, a_lambda_k1, a_lambda_q2, a_lambda_k2,
              a_subln, b_rpb, c_q_norm, c_kv_norm, c_w_uq, c_w_ukv, d_q_norm, d_k_norm,
              w_branch, w_out, ffn_norm, w_ffn_in, ffn_conv_w, ffn_conv_b, w_ffn_out, final_norm):
    params = (attn_norm, w_in, a_lambda_q1, a_lambda_k1, a_lambda_q2, a_lambda_k2, a_subln,
              b_rpb, c_q_norm, c_kv_norm, c_w_uq, c_w_ukv, d_q_norm, d_k_norm,
              w_branch, w_out, ffn_norm, w_ffn_in, ffn_conv_w, ffn_conv_b, w_ffn_out, final_norm)
    y_prompt = encoder_trunk(x_prompt, *params)
    y_sample = encoder_trunk(x_sample, *params)
    return (y_prompt, y_sample)
```

```python
import functools
import math

import numpy as np
import jax
import jax.numpy as jnp
from jax import lax
from jax.experimental import pallas as pl
from jax.experimental.pallas import tpu as pltpu

F32 = jnp.float32
BF16 = jnp.bfloat16

D_MODEL = 1024
SEQ = 2048
DEPTH = 2
GRID_W = 64
ROPE_THETA = 10000.0
EPS = 1e-6
N_BRANCH = 4
HEADS = 4
HEAD_DIM = 64
A_D = 32
WIN_R = 8
WIN_C = 16
C_NOPE = 64
C_ROPE = 32
C_QLORA = 192
C_KVLORA = 128
D_KV_HEADS = 2
BRANCH_W = 256
FFN_DIM = 2816
LANES = 128
ROPE_HALF = 16

MIX_COLS = 2560
TM_IN = 512
TM_MERGE = 512
TQ = 256
NB_Q = 2 * GRID_W
NB_KEYS = 10 * GRID_W
NB_STEPS = SEQ // NB_Q
NB_VARIANTS = 5
NEG = -1e30
FFN_CHUNK = 256
N_FFN_CHUNKS = FFN_DIM // FFN_CHUNK
VMEM_LIMIT = 56 * 1024 * 1024


def _rms(x, g):
    return x * lax.rsqrt(jnp.mean(x * x, axis=-1, keepdims=True) + EPS) * g


def _rope_slab(x, cos, sin_lo, sin_hi):
    return (x * cos + pltpu.roll(x, LANES - ROPE_HALF, 1) * sin_lo
            + pltpu.roll(x, ROPE_HALF, 1) * sin_hi)


def _group_mean_sq(x, width):
    sq = x * x
    hi = sq.astype(BF16)
    lo = (sq - hi.astype(F32)).astype(BF16)
    r = lax.broadcasted_iota(jnp.int32, (LANES, LANES), 0) // width
    c = lax.broadcasted_iota(jnp.int32, (LANES, LANES), 1) // width
    ones = jnp.where(r == c, 1.0, 0.0).astype(BF16)
    tot = (jnp.dot(hi, ones, preferred_element_type=F32)
           + jnp.dot(lo, ones, preferred_element_type=F32))
    return tot * (1.0 / width)


def _inproj_kernel(x_ref, g_ref, w_ref, tab_ref, cqn_ref, ckvn_ref, wuq_ref, wukv_ref, dqn_ref, dkn_ref,
                   qa_ref, kat_ref, va_ref, qb_ref, kb_ref, vb_ref,
                   qc_ref, kct_ref, vc_ref, qd_ref, kdt_ref, vd_ref):
    h = _rms(x_ref[...], g_ref[...]).astype(BF16)
    proj = jnp.dot(h, w_ref[...], preferred_element_type=F32)

    def slab(i):
        return proj[:, i * LANES:(i + 1) * LANES]

    def rope(x, variant):
        return _rope_slab(x, tab_ref[3 * variant], tab_ref[3 * variant + 1], tab_ref[3 * variant + 2])

    for i in range(2):
        qa_ref[:, i * LANES:(i + 1) * LANES] = (rope(slab(i), 0) * (A_D ** -0.5)).astype(BF16)
        kat_ref[i * LANES:(i + 1) * LANES, :] = rope(slab(2 + i), 0).T.astype(BF16)
    va_ref[...] = proj[:, 512:768].astype(BF16)
    qb_ref[...] = (proj[:, 768:1024] * (HEAD_DIM ** -0.5)).astype(BF16)
    kb_ref[...] = proj[:, 1024:1280].astype(BF16)
    vb_ref[...] = proj[:, 1280:1536].astype(BF16)
    for i in range(2):
        x = slab(12 + i)
        x = x * lax.rsqrt(_group_mean_sq(x, HEAD_DIM) + EPS) * dqn_ref[...]
        qd_ref[:, i * LANES:(i + 1) * LANES] = (rope(x, 2) * (HEAD_DIM ** -0.5)).astype(BF16)
    x = slab(14)
    x = x * lax.rsqrt(_group_mean_sq(x, HEAD_DIM) + EPS) * dkn_ref[...]
    kdt_ref[...] = rope(x, 2).T.astype(BF16)
    vd_ref[...] = slab(15).astype(BF16)
    cq = proj[:, 2048:2304]
    cq = cq * lax.rsqrt(jnp.sum(cq * cq, axis=-1, keepdims=True) * (1.0 / C_QLORA) + EPS) * cqn_ref[...]
    q = jnp.dot(cq.astype(BF16), wuq_ref[...], preferred_element_type=F32)
    ckv = slab(18)
    ckv = _rms(ckv, ckvn_ref[...])
    kv = jnp.dot(ckv.astype(BF16), wukv_ref[...], preferred_element_type=F32)
    kr = rope(slab(19), 1)
    for i in range(HEADS):
        qc_ref[:, i * LANES:(i + 1) * LANES] = (
            rope(q[:, i * LANES:(i + 1) * LANES], 1) * ((C_NOPE + C_ROPE) ** -0.5)).astype(BF16)
        kct_ref[i * LANES:(i + 1) * LANES, :] = (kv[:, i * LANES:(i + 1) * LANES] + kr).T.astype(BF16)
    vc_ref[...] = kv[:, 512:768].astype(BF16)


def _inproj(x, g, w_mix, tables, cqn, ckvn, wuq, wukv, dqn, dkn):
    b = x.shape[0]
    tm = TM_IN
    grid = (SEQ // tm, b)
    tok = lambda c: pl.BlockSpec((None, tm, c), lambda j, i: (i, j, 0))
    tr = lambda c: pl.BlockSpec((None, c, tm), lambda j, i: (i, 0, j))
    full = lambda a: pl.BlockSpec(a.shape, lambda j, i: (0,) * a.ndim)
    out_cols = [("tok", 256), ("tr", 256), ("tok", 256), ("tok", 256), ("tok", 256), ("tok", 256),
                ("tok", 512), ("tr", 512), ("tok", 256), ("tok", 256), ("tr", 128), ("tok", 128)]
    out_specs = [tok(c) if k == "tok" else tr(c) for k, c in out_cols]
    out_shape = [jax.ShapeDtypeStruct((b, SEQ, c) if k == "tok" else (b, c, SEQ), BF16) for k, c in out_cols]
    return pl.pallas_call(
        _inproj_kernel,
        grid=grid,
        in_specs=[tok(D_MODEL), full(g), full(w_mix),
                  pl.BlockSpec((9, tm, LANES), lambda j, i: (0, j, 0)),
                  full(cqn), full(ckvn), full(wuq), full(wukv), full(dqn), full(dkn)],
        out_specs=out_specs,
        out_shape=out_shape,
        compiler_params=pltpu.CompilerParams(
            dimension_semantics=("arbitrary", "arbitrary"), vmem_limit_bytes=VMEM_LIMIT),
        name="inproj",
    )(x, g, w_mix, tables, cqn, ckvn, wuq, wukv, dqn, dkn)


def _softmax_terms(s):
    m = jnp.max(s, axis=-1, keepdims=True)
    e = jnp.exp(s - m)
    return e, jnp.sum(e, axis=-1, keepdims=True)


def _lane_half_mask(rows, half):
    lane = lax.broadcasted_iota(jnp.int32, (rows, LANES), 1)
    return (lane >= HEAD_DIM * half) & (lane < HEAD_DIM * (half + 1))


def _attn_a_kernel(lam_ref, q_ref, kt_ref, v_ref, g_ref, o_ref, *, out_scale):
    lam = lam_ref[0]
    tq = q_ref.shape[0]
    for pair in range(2):
        ys = []
        for half in range(2):
            head = 2 * pair + half
            terms = []
            for c in range(2):
                j = 2 * head + c
                s = jnp.dot(q_ref[:, A_D * j:A_D * (j + 1)], kt_ref[A_D * j:A_D * (j + 1), :],
                            preferred_element_type=F32)
                terms.append(_softmax_terms(s))
            (e0, l0), (e1, l1) = terms
            w = (e0 * (1.0 / l0) - e1 * (lam / l1)).astype(BF16)
            o = jnp.dot(w, v_ref[:, pair * LANES:(pair + 1) * LANES], preferred_element_type=F32)
            mask = _lane_half_mask(tq, half)
            ms = jnp.sum(jnp.where(mask, o * o, 0.0), axis=-1, keepdims=True) * (1.0 / HEAD_DIM)
            ys.append(o * lax.rsqrt(ms + EPS))
        y = jnp.where(_lane_half_mask(tq, 0), ys[0], ys[1])
        o_ref[:, pair * LANES:(pair + 1) * LANES] = (y * g_ref[...] * out_scale).astype(BF16)


def _attn_c_kernel(q_ref, kt_ref, v_ref, o_ref):
    tq = q_ref.shape[0]
    for pair in range(2):
        ys = []
        for half in range(2):
            head = 2 * pair + half
            s = jnp.dot(q_ref[:, head * LANES:(head + 1) * LANES], kt_ref[head * LANES:(head + 1) * LANES, :],
                        preferred_element_type=F32)
            e, l = _softmax_terms(s)
            o = jnp.dot(e.astype(BF16), v_ref[:, pair * LANES:(pair + 1) * LANES], preferred_element_type=F32)
            ys.append(o * (1.0 / l))
        o_ref[:, pair * LANES:(pair + 1) * LANES] = jnp.where(_lane_half_mask(tq, 0), ys[0], ys[1]).astype(BF16)


def _attn_d_kernel(q_ref, kt_ref, v_ref, o_ref):
    tq = q_ref.shape[0]
    for kv_head in range(D_KV_HEADS):
        ys = []
        for g in range(2):
            head = 2 * kv_head + g
            s = jnp.dot(q_ref[:, head * HEAD_DIM:(head + 1) * HEAD_DIM],
                        kt_ref[kv_head * HEAD_DIM:(kv_head + 1) * HEAD_DIM, :], preferred_element_type=F32)
            e, l = _softmax_terms(s)
            o = jnp.dot(e.astype(BF16), v_ref[...], preferred_element_type=F32) * (1.0 / l)
            if g != kv_head:
                o = pltpu.roll(o, HEAD_DIM, 1)
            ys.append(o)
        o_ref[:, kv_head * LANES:(kv_head + 1) * LANES] = (
            jnp.where(_lane_half_mask(tq, 0), ys[0], ys[1]).astype(BF16))


def _attn_full(kernel, q, kt, v, extra_in=(), extra_specs=(), scalar=None, name=None):
    b = q.shape[0]
    grid = (b, SEQ // TQ)
    in_specs = [pl.BlockSpec((None, TQ, q.shape[2]), lambda i, j: (i, j, 0)),
                pl.BlockSpec((None, kt.shape[1], SEQ), lambda i, j: (i, 0, 0)),
                pl.BlockSpec((None, SEQ, v.shape[2]), lambda i, j: (i, 0, 0))]
    args = [q, kt, v]
    if scalar is not None:
        in_specs = [pl.BlockSpec(memory_space=pltpu.SMEM)] + in_specs
        args = [scalar] + args
    in_specs += list(extra_specs)
    args += list(extra_in)
    return pl.pallas_call(
        kernel,
        grid=grid,
        in_specs=in_specs,
        out_specs=pl.BlockSpec((None, TQ, BRANCH_W), lambda i, j: (i, j, 0)),
        out_shape=jax.ShapeDtypeStruct((b, SEQ, BRANCH_W), BF16),
        compiler_params=pltpu.CompilerParams(
            dimension_semantics=("arbitrary", "arbitrary"), vmem_limit_bytes=VMEM_LIMIT),
        name=name,
    )(*args)


def _nb_window_start(step):
    return jnp.clip(step - 2, 0, NB_STEPS - 5) * 2


def _attn_b_kernel(q_ref, k_ref, v_ref, bias_ref, o_ref):
    start = pl.multiple_of(_nb_window_start(pl.program_id(1)) * GRID_W, LANES)
    kw = k_ref[pl.ds(start, NB_KEYS), :]
    vw = v_ref[pl.ds(start, NB_KEYS), :]
    for pair in range(2):
        ys = []
        for half in range(2):
            head = 2 * pair + half
            s = lax.dot_general(q_ref[:, head * HEAD_DIM:(head + 1) * HEAD_DIM],
                                kw[:, head * HEAD_DIM:(head + 1) * HEAD_DIM],
                                (((1,), (1,)), ((), ())), preferred_element_type=F32)
            e, l = _softmax_terms(s + bias_ref[head])
            o = jnp.dot(e.astype(BF16), vw[:, pair * LANES:(pair + 1) * LANES], preferred_element_type=F32)
            ys.append(o * (1.0 / l))
        o_ref[:, pair * LANES:(pair + 1) * LANES] = jnp.where(_lane_half_mask(NB_Q, 0), ys[0], ys[1]).astype(BF16)


def _nb_variant(step):
    return jnp.minimum(step, 2) + jnp.maximum(step - (NB_STEPS - 3), 0)


def _attn_b(q, k, v, bias):
    b = q.shape[0]
    return pl.pallas_call(
        _attn_b_kernel,
        grid=(b, NB_STEPS),
        in_specs=[pl.BlockSpec((None, NB_Q, BRANCH_W), lambda i, j: (i, j, 0)),
                  pl.BlockSpec((None, SEQ, BRANCH_W), lambda i, j: (i, 0, 0)),
                  pl.BlockSpec((None, SEQ, BRANCH_W), lambda i, j: (i, 0, 0)),
                  pl.BlockSpec((None, HEADS, NB_Q, NB_KEYS), lambda i, j: (_nb_variant(j), 0, 0, 0))],
        out_specs=pl.BlockSpec((None, NB_Q, BRANCH_W), lambda i, j: (i, j, 0)),
        out_shape=jax.ShapeDtypeStruct((b, SEQ, BRANCH_W), BF16),
        compiler_params=pltpu.CompilerParams(
            dimension_semantics=("arbitrary", "arbitrary"), vmem_limit_bytes=VMEM_LIMIT),
        name="attn_b",
    )(q, k, v, bias)


def _nb_bias_tables(rpb):
    rows = SEQ // GRID_W
    steps = np.array([0, 1, 2, NB_STEPS - 2, NB_STEPS - 1])
    start_row = np.clip(steps - 2, 0, NB_STEPS - 5) * 2
    t = np.arange(NB_Q)
    r = 2 * steps[:, None] + t[None, :] // GRID_W
    qc = np.broadcast_to(t % GRID_W, r.shape)
    j = np.arange(NB_KEYS)
    krow = start_row[:, None] + j[None, :] // GRID_W
    kcol = np.broadcast_to(j % GRID_W, krow.shape)
    rs = np.clip(r - WIN_R // 2, 0, rows - WIN_R)
    cs = np.clip(qc - WIN_C // 2, 0, GRID_W - WIN_C)
    kr = krow[:, None, :]
    kc = kcol[:, None, :]
    valid = ((kr >= rs[..., None]) & (kr < rs[..., None] + WIN_R)
             & (kc >= cs[..., None]) & (kc < cs[..., None] + WIN_C))
    dr = np.clip(kr - r[..., None] + WIN_R - 1, 0, 2 * WIN_R - 2)
    dc = np.clip(kc - qc[..., None] + WIN_C - 1, 0, 2 * WIN_C - 2)
    bias = rpb[:, dr, dc].astype(F32)
    bias = jnp.where(jnp.asarray(valid)[None], bias, NEG)
    return jnp.transpose(bias, (1, 0, 2, 3))


def _merge_kernel(x_ref, g_ref, oa_ref, ob_ref, oc_ref, od_ref, wg_ref, wb_ref, wo_ref, y_ref):
    x = x_ref[...]
    h = _rms(x, g_ref[...]).astype(BF16)
    merged = None
    for i, o_ref in enumerate((oa_ref, ob_ref, oc_ref, od_ref)):
        gate = jax.nn.sigmoid(jnp.dot(h, wg_ref[:, i * D_MODEL:(i + 1) * D_MODEL], preferred_element_type=F32))
        term = gate * jnp.dot(o_ref[...], wb_ref[i], preferred_element_type=F32)
        merged = term if merged is None else merged + term
    y_ref[...] = x + jnp.dot(merged.astype(BF16), wo_ref[...], preferred_element_type=F32)


def _merge(x, g, oa, ob, oc, od, wg, wb, wo):
    t = x.shape[0]
    tm = TM_MERGE
    tok = lambda c: pl.BlockSpec((tm, c), lambda i: (i, 0))
    full = lambda a: pl.BlockSpec(a.shape, lambda i: (0,) * a.ndim)
    return pl.pallas_call(
        _merge_kernel,
        grid=(t // tm,),
        in_specs=[tok(D_MODEL), full(g), tok(BRANCH_W), tok(BRANCH_W), tok(BRANCH_W), tok(BRANCH_W),
                  full(wg), full(wb), full(wo)],
        out_specs=tok(D_MODEL),
        out_shape=jax.ShapeDtypeStruct((t, D_MODEL), F32),
        compiler_params=pltpu.CompilerParams(
            dimension_semantics=("arbitrary",), vmem_limit_bytes=VMEM_LIMIT),
        name="merge",
    )(x, g, oa, ob, oc, od, wg, wb, wo)


def _ffn_kernel(x_ref, g_ref, wu_ref, wgt_ref, cw_ref, cb_ref, wo_ref, fg_ref, y_ref, h_scr, gpad, *, final):
    c = pl.program_id(1)

    @pl.when(c == 0)
    def _():
        x = x_ref[...]
        h_scr[...] = _rms(x, g_ref[...]).astype(BF16)
        y_ref[...] = x
        gpad[0:8, :] = jnp.zeros((8, FFN_CHUNK), F32)
        gpad[SEQ + 8:SEQ + 16, :] = jnp.zeros((8, FFN_CHUNK), F32)

    h = h_scr[...]
    u = jnp.dot(h, wu_ref[...], preferred_element_type=F32)
    gpad[8:SEQ + 8, :] = jnp.dot(h, wgt_ref[...], preferred_element_type=F32)
    gc = (gpad[7:SEQ + 7, :] * cw_ref[0:1, :] + gpad[8:SEQ + 8, :] * cw_ref[1:2, :]
          + gpad[9:SEQ + 9, :] * cw_ref[2:3, :] + cb_ref[...])
    act = (jax.nn.silu(gc) * u).astype(BF16)
    y_ref[...] += jnp.dot(act, wo_ref[...], preferred_element_type=F32)

    if final:
        @pl.when(c == N_FFN_CHUNKS - 1)
        def _():
            y_ref[...] = _rms(y_ref[...], fg_ref[...])


def _ffn(x, g, w_in, conv_w, conv_b, w_out, final_g, final):
    b = x.shape[0]
    return pl.pallas_call(
        functools.partial(_ffn_kernel, final=final),
        grid=(b, N_FFN_CHUNKS),
        in_specs=[pl.BlockSpec((None, SEQ, D_MODEL), lambda i, c: (i, 0, 0), pipeline_mode=pl.Buffered(1)),
                  pl.BlockSpec((1, D_MODEL), lambda i, c: (0, 0)),
                  pl.BlockSpec((D_MODEL, FFN_CHUNK), lambda i, c: (0, c)),
                  pl.BlockSpec((D_MODEL, FFN_CHUNK), lambda i, c: (0, N_FFN_CHUNKS + c)),
                  pl.BlockSpec((3, FFN_CHUNK), lambda i, c: (0, c)),
                  pl.BlockSpec((1, FFN_CHUNK), lambda i, c: (0, c)),
                  pl.BlockSpec((FFN_CHUNK, D_MODEL), lambda i, c: (c, 0)),
                  pl.BlockSpec((1, D_MODEL), lambda i, c: (0, 0))],
        out_specs=pl.BlockSpec((None, SEQ, D_MODEL), lambda i, c: (i, 0, 0)),
        out_shape=jax.ShapeDtypeStruct((b, SEQ, D_MODEL), F32),
        scratch_shapes=[pltpu.VMEM((SEQ, D_MODEL), BF16), pltpu.VMEM((SEQ + 16, FFN_CHUNK), F32)],
        compiler_params=pltpu.CompilerParams(
            dimension_semantics=("arbitrary", "arbitrary"), vmem_limit_bytes=VMEM_LIMIT),
        name="ffn",
    )(x, g, w_in, w_in, conv_w, conv_b, w_out, final_g)


def _rope_lane_tables():
    t = jnp.arange(SEQ)
    inv_freq = ROPE_THETA ** (-jnp.arange(0, 2 * ROPE_HALF, 2, dtype=F32) / (2 * ROPE_HALF))
    ang = lambda pos: pos.astype(F32)[:, None] * inv_freq[None, :]
    lane = np.arange(LANES)
    freq = lane % ROPE_HALF
    first = jnp.asarray((lane % (2 * ROPE_HALF)) < ROPE_HALF)[None, :]
    ang_seq = ang(t)[:, freq]
    ang_axial = jnp.where(jnp.asarray((lane // (2 * ROPE_HALF)) % 2 == 0)[None, :],
                          ang(t // GRID_W)[:, freq], ang(t % GRID_W)[:, freq])
    active = jnp.asarray((lane >= C_NOPE) & (lane < C_NOPE + C_ROPE))[None, :]

    def trio(a, on):
        cos, sin = jnp.cos(a), jnp.sin(a)
        return [jnp.where(on, cos, 1.0), jnp.where(on & first, -sin, 0.0), jnp.where(on & ~first, sin, 0.0)]

    everywhere = jnp.ones((1, LANES), bool)
    return jnp.stack(trio(ang_seq, everywhere) + trio(ang_seq, active) + trio(ang_axial, everywhere))


def _pack_layer(l, w_in, c_q_norm, c_kv_norm, c_w_uq, c_w_ukv, d_q_norm, d_k_norm):
    w = w_in[l]
    a, bb, cc, dd, gate = jnp.split(w, [768, 1536, 1888, 2400], axis=1)
    z = lambda n: jnp.zeros((D_MODEL, n), F32)
    cq, ckv, krope = cc[:, :C_QLORA], cc[:, C_QLORA:C_QLORA + C_KVLORA], cc[:, C_QLORA + C_KVLORA:]
    w_mix = jnp.concatenate([a, bb, dd, cq, z(64), ckv, z(C_NOPE), krope, z(LANES - C_NOPE - C_ROPE)], axis=1)
    assert w_mix.shape[1] == MIX_COLS
    wuq = c_w_uq[l].reshape(C_QLORA, HEADS, C_NOPE + C_ROPE)
    wuq = jnp.pad(wuq, ((0, 256 - C_QLORA), (0, 0), (0, LANES - C_NOPE - C_ROPE))).reshape(256, HEADS * LANES)
    wukv = c_w_ukv[l].reshape(C_KVLORA, HEADS, C_NOPE + HEAD_DIM)
    wk = jnp.pad(wukv[:, :, :C_NOPE], ((0, 0), (0, 0), (0, LANES - C_NOPE))).reshape(C_KVLORA, HEADS * LANES)
    wv = wukv[:, :, C_NOPE:].reshape(C_KVLORA, HEADS * HEAD_DIM)
    wukv = jnp.concatenate([wk, wv], axis=1)
    cqn = jnp.pad(c_q_norm[l], (0, 256 - C_QLORA))[None, :]
    return dict(
        w_mix=w_mix.astype(BF16), w_gate=gate.astype(BF16), wuq=wuq.astype(BF16), wukv=wukv.astype(BF16),
        cqn=cqn, ckvn=c_kv_norm[l][None, :],
        dqn=jnp.tile(d_q_norm[l], 2)[None, :], dkn=jnp.tile(d_k_norm[l], 2)[None, :])


def _trunk(x, tables, layers, final_norm):
    b = x.shape[0]
    for l, p in enumerate(layers):
        (qa, kat, va, qb, kb, vb, qc, kct, vc, qd, kdt, vd) = _inproj(
            x, p["attn_norm"], p["w_mix"], tables, p["cqn"], p["ckvn"], p["wuq"], p["wukv"], p["dqn"], p["dkn"])
        lam_init = 0.8 - 0.6 * math.exp(-0.3 * l)
        oa = _attn_full(functools.partial(_attn_a_kernel, out_scale=1.0 - lam_init), qa, kat, va,
                        extra_in=(p["subln"],), extra_specs=(pl.BlockSpec((1, LANES), lambda i, j: (0, 0)),),
                        scalar=p["lam"], name="attn_a")
        ob = _attn_b(qb, kb, vb, p["nb_bias"])
        oc = _attn_full(_attn_c_kernel, qc, kct, vc, name="attn_c")
        od = _attn_full(_attn_d_kernel, qd, kdt, vd, name="attn_d")
        flat = lambda a: a.reshape(b * SEQ, a.shape[-1])
        x1 = _merge(flat(x), p["attn_norm"], flat(oa), flat(ob), flat(oc), flat(od),
                    p["w_gate"], p["w_branch"], p["w_out"]).reshape(b, SEQ, D_MODEL)
        x = _ffn(x1, p["ffn_norm"], p["w_ffn_in"], p["conv_w"], p["conv_b"], p["w_ffn_out"],
                 final_norm, final=(l == DEPTH - 1))
    return x


def kernel(x_prompt, x_sample, attn_norm, w_in, a_lambda_q1, a_lambda_k1, a_lambda_q2, a_lambda_k2, a_subln, b_rpb, c_q_norm, c_kv_norm, c_w_uq, c_w_ukv, d_q_norm, d_k_norm, w_branch, w_out, ffn_norm, w_ffn_in, ffn_conv_w, ffn_conv_b, w_ffn_out, final_norm):
    tables = _rope_lane_tables()
    layers = []
    for l in range(DEPTH):
        p = _pack_layer(l, w_in, c_q_norm, c_kv_norm, c_w_uq, c_w_ukv, d_q_norm, d_k_norm)
        lam_init = 0.8 - 0.6 * math.exp(-0.3 * l)
        lam = (jnp.exp(jnp.sum(a_lambda_q1[l] * a_lambda_k1[l]))
               - jnp.exp(jnp.sum(a_lambda_q2[l] * a_lambda_k2[l])) + lam_init)
        p.update(
            attn_norm=attn_norm[l][None, :], lam=lam.reshape(1).astype(F32),
            subln=jnp.tile(a_subln[l], 2)[None, :], nb_bias=_nb_bias_tables(b_rpb[l]),
            w_branch=w_branch[l].astype(BF16), w_out=w_out[l].astype(BF16),
            ffn_norm=ffn_norm[l][None, :], w_ffn_in=w_ffn_in[l].astype(BF16),
            conv_w=ffn_conv_w[l], conv_b=ffn_conv_b[l][None, :], w_ffn_out=w_ffn_out[l].astype(BF16))
        layers.append(p)
    fn = final_norm[None, :]
    return (_trunk(x_prompt, tables, layers, fn), _trunk(x_sample, tables, layers, fn))
```

```python
import functools
import math

import numpy as np
import jax
import jax.numpy as jnp
from jax import lax
from jax.experimental import pallas as pl
from jax.experimental.pallas import tpu as pltpu

F32 = jnp.float32
BF16 = jnp.bfloat16

D_MODEL = 1024
SEQ = 2048
DEPTH = 2
GRID_W = 64
ROPE_THETA = 10000.0
EPS = 1e-6
N_BRANCH = 4
HEADS = 4
HEAD_DIM = 64
A_D = 32
WIN_R = 8
WIN_C = 16
C_NOPE = 64
C_ROPE = 32
C_QLORA = 192
C_KVLORA = 128
D_KV_HEADS = 2
BRANCH_W = 256
FFN_DIM = 2816
LANES = 128
SUBLANES = 8
ROPE_HALF = 16
LOG2E = 1.4426950408889634

MIX_COLS = 2560
TM_IN = 512
TM_MERGE = 512
TQ = 256
TILES_PER_ITER = 2
NB_STEPS_PER_ITER = 2
NB_Q = 2 * GRID_W
NB_KEYS = 10 * GRID_W
NB_STEPS = SEQ // NB_Q
NB_VARIANTS = 5
NEG = -1e30
TM_FFN = 512
FFN_CHUNK = 256
N_FFN_CHUNKS = FFN_DIM // FFN_CHUNK
VMEM_LIMIT = 56 * 1024 * 1024


def _rms(x, g):
    return x * lax.rsqrt(jnp.mean(x * x, axis=-1, keepdims=True) + EPS) * g


def _rope_slab(x, cos, sin_lo, sin_hi):
    return (x * cos + pltpu.roll(x, LANES - ROPE_HALF, 1) * sin_lo
            + pltpu.roll(x, ROPE_HALF, 1) * sin_hi)


def _group_mean_sq(x, width):
    sq = x * x
    hi = sq.astype(BF16)
    lo = (sq - hi.astype(F32)).astype(BF16)
    r = lax.broadcasted_iota(jnp.int32, (LANES, LANES), 0) // width
    c = lax.broadcasted_iota(jnp.int32, (LANES, LANES), 1) // width
    ones = jnp.where(r == c, 1.0, 0.0).astype(BF16)
    tot = (jnp.dot(hi, ones, preferred_element_type=F32)
           + jnp.dot(lo, ones, preferred_element_type=F32))
    return tot * (1.0 / width)


def _low_half(rows):
    return lax.broadcasted_iota(jnp.int32, (rows, LANES), 1) < HEAD_DIM


def _value_with_ones(v, shift):
    if shift:
        v = pltpu.roll(v, HEAD_DIM, 1)
    return jnp.where(_low_half(v.shape[0]), v, 1.0).astype(BF16)


def _inproj_kernel(x_ref, g_ref, w_ref, tab_ref, cqn_ref, ckvn_ref, wuq_ref, wukv_ref, dqn_ref, dkn_ref,
                   qa_ref, kat_ref, va_ref, qb_ref, kb_ref, vb_ref,
                   qc_ref, kct_ref, vc_ref, qd_ref, kdt_ref, vd_ref):
    h = _rms(x_ref[...], g_ref[...]).astype(BF16)
    proj = jnp.dot(h, w_ref[...], preferred_element_type=F32)

    def slab(i):
        return proj[:, i * LANES:(i + 1) * LANES]

    def rope(x, variant):
        return _rope_slab(x, tab_ref[3 * variant], tab_ref[3 * variant + 1], tab_ref[3 * variant + 2])

    for i in range(2):
        qa_ref[:, i * LANES:(i + 1) * LANES] = (rope(slab(i), 0) * (A_D ** -0.5 * LOG2E)).astype(BF16)
        kat_ref[i * LANES:(i + 1) * LANES, :] = rope(slab(2 + i), 0).T.astype(BF16)
    va_ref[...] = proj[:, 512:768].astype(BF16)
    qb_ref[...] = (proj[:, 768:1024] * (HEAD_DIM ** -0.5 * LOG2E)).astype(BF16)
    kb_ref[...] = proj[:, 1024:1280].astype(BF16)
    for i in range(HEADS):
        vb_ref[:, i * LANES:(i + 1) * LANES] = _value_with_ones(slab(10 + i // 2), i % 2)
    for i in range(2):
        x = slab(12 + i)
        x = x * lax.rsqrt(_group_mean_sq(x, HEAD_DIM) + EPS) * dqn_ref[...]
        qd_ref[:, i * LANES:(i + 1) * LANES] = (rope(x, 2) * (HEAD_DIM ** -0.5 * LOG2E)).astype(BF16)
    x = slab(14)
    x = x * lax.rsqrt(_group_mean_sq(x, HEAD_DIM) + EPS) * dkn_ref[...]
    kdt_ref[...] = rope(x, 2).T.astype(BF16)
    for i in range(D_KV_HEADS):
        vd_ref[:, i * LANES:(i + 1) * LANES] = _value_with_ones(slab(15), i)
    cq = proj[:, 2048:2304]
    cq = cq * lax.rsqrt(jnp.sum(cq * cq, axis=-1, keepdims=True) * (1.0 / C_QLORA) + EPS) * cqn_ref[...]
    q = jnp.dot(cq.astype(BF16), wuq_ref[...], preferred_element_type=F32)
    ckv = _rms(slab(18), ckvn_ref[...])
    kv = jnp.dot(ckv.astype(BF16), wukv_ref[...], preferred_element_type=F32)
    kr = rope(slab(19), 1)
    for i in range(HEADS):
        qc_ref[:, i * LANES:(i + 1) * LANES] = (
            rope(q[:, i * LANES:(i + 1) * LANES], 1) * ((C_NOPE + C_ROPE) ** -0.5 * LOG2E)).astype(BF16)
        kct_ref[i * LANES:(i + 1) * LANES, :] = (kv[:, i * LANES:(i + 1) * LANES] + kr).T.astype(BF16)
        vc_ref[:, i * LANES:(i + 1) * LANES] = _value_with_ones(kv[:, (HEADS + i) * LANES:(HEADS + i + 1) * LANES], 0)


def _inproj(x, g, w_mix, tables, cqn, ckvn, wuq, wukv, dqn, dkn):
    b = x.shape[0]
    tm = TM_IN
    grid = (SEQ // tm, b)
    tok = lambda c: pl.BlockSpec((None, tm, c), lambda j, i: (i, j, 0))
    tr = lambda c: pl.BlockSpec((None, c, tm), lambda j, i: (i, 0, j))
    full = lambda a: pl.BlockSpec(a.shape, lambda j, i: (0,) * a.ndim)
    out_cols = [("tok", 256), ("tr", 256), ("tok", 256), ("tok", 256), ("tok", 256), ("tok", 512),
                ("tok", 512), ("tr", 512), ("tok", 512), ("tok", 256), ("tr", 128), ("tok", 256)]
    out_specs = [tok(c) if k == "tok" else tr(c) for k, c in out_cols]
    out_shape = [jax.ShapeDtypeStruct((b, SEQ, c) if k == "tok" else (b, c, SEQ), BF16) for k, c in out_cols]
    return pl.pallas_call(
        _inproj_kernel,
        grid=grid,
        in_specs=[tok(D_MODEL), full(g), full(w_mix),
                  pl.BlockSpec((9, tm, LANES), lambda j, i: (0, j, 0)),
                  full(cqn), full(ckvn), full(wuq), full(wukv), full(dqn), full(dkn)],
        out_specs=out_specs,
        out_shape=out_shape,
        compiler_params=pltpu.CompilerParams(
            dimension_semantics=("arbitrary", "arbitrary"), vmem_limit_bytes=VMEM_LIMIT),
        name="inproj",
    )(x, g, w_mix, tables, cqn, ckvn, wuq, wukv, dqn, dkn)


def _exp2_terms(s):
    return jnp.exp2(s - jnp.max(s, axis=-1, keepdims=True))


def _normalise_by_ones_lanes(o):
    return o * (1.0 / pltpu.roll(o, HEAD_DIM, 1))


def _join_halves(y_low, y_high):
    return jnp.where(_low_half(y_low.shape[0]), y_low, pltpu.roll(y_high, HEAD_DIM, 1))


def _lookahead(n, scores, finish):
    s = scores(0)
    for j in range(n):
        s_next = scores(j + 1) if j + 1 < n else None
        finish(j, s)
        s = s_next


def _tile_rows(i, t):
    return pl.ds(pl.multiple_of((i * TILES_PER_ITER + t) * TQ, TQ), TQ)


def _attn_a_kernel(lam_ref, q_ref, kt_ref, v_ref, g_ref, o_ref, *, out_scale):
    lam = lam_ref[0]
    per_tile = 2 * HEADS

    def tiles(i, carry):
        def scores(job):
            t, j = divmod(job, per_tile)
            return jnp.dot(q_ref[_tile_rows(i, t), A_D * j:A_D * (j + 1)], kt_ref[A_D * j:A_D * (j + 1), :],
                           preferred_element_type=F32)

        held = {}

        def finish(job, s):
            t, j = divmod(job, per_tile)
            head, c = divmod(j, 2)
            pair, half = divmod(head, 2)
            e = _exp2_terms(s)
            l = jnp.sum(e, axis=-1, keepdims=True)
            if c == 0:
                held["first"] = (e, l)
                return
            e0, l0 = held.pop("first")
            w = (e0 - e * (lam * l0 / l)).astype(BF16)
            o = jnp.dot(w, v_ref[:, pair * LANES:(pair + 1) * LANES], preferred_element_type=F32) * (1.0 / l0)
            mask = _low_half(TQ) if half == 0 else ~_low_half(TQ)
            ms = jnp.sum(jnp.where(mask, o * o, 0.0), axis=-1, keepdims=True) * (1.0 / HEAD_DIM)
            y = o * lax.rsqrt(ms + EPS)
            if half == 0:
                held["low"] = y
                return
            y = jnp.where(_low_half(TQ), held.pop("low"), y)
            o_ref[_tile_rows(i, t), pair * LANES:(pair + 1) * LANES] = (y * g_ref[...] * out_scale).astype(BF16)

        _lookahead(TILES_PER_ITER * per_tile, scores, finish)
        return carry

    lax.fori_loop(0, SEQ // (TQ * TILES_PER_ITER), tiles, 0)


def _attn_c_kernel(q_ref, kt_ref, v_ref, o_ref):
    def tiles(i, carry):
        def scores(job):
            t, head = divmod(job, HEADS)
            cols = slice(head * LANES, (head + 1) * LANES)
            return jnp.dot(q_ref[_tile_rows(i, t), cols], kt_ref[cols, :], preferred_element_type=F32)

        held = {}

        def finish(job, s):
            t, head = divmod(job, HEADS)
            pair, half = divmod(head, 2)
            y = _normalise_by_ones_lanes(
                jnp.dot(_exp2_terms(s).astype(BF16), v_ref[:, head * LANES:(head + 1) * LANES],
                        preferred_element_type=F32))
            if half == 0:
                held["low"] = y
                return
            o_ref[_tile_rows(i, t), pair * LANES:(pair + 1) * LANES] = _join_halves(held.pop("low"), y).astype(BF16)

        _lookahead(TILES_PER_ITER * HEADS, scores, finish)
        return carry

    lax.fori_loop(0, SEQ // (TQ * TILES_PER_ITER), tiles, 0)


def _attn_d_kernel(q_ref, kt_ref, v_ref, o_ref):
    def tiles(i, carry):
        def scores(job):
            t, head = divmod(job, HEADS)
            kv_head = head // 2
            return jnp.dot(q_ref[_tile_rows(i, t), head * HEAD_DIM:(head + 1) * HEAD_DIM],
                           kt_ref[kv_head * HEAD_DIM:(kv_head + 1) * HEAD_DIM, :], preferred_element_type=F32)

        held = {}

        def finish(job, s):
            t, head = divmod(job, HEADS)
            kv_head, g = divmod(head, 2)
            y = _normalise_by_ones_lanes(
                jnp.dot(_exp2_terms(s).astype(BF16), v_ref[:, kv_head * LANES:(kv_head + 1) * LANES],
                        preferred_element_type=F32))
            if g == 0:
                held["low"] = y
                return
            o_ref[_tile_rows(i, t), kv_head * LANES:(kv_head + 1) * LANES] = (
                _join_halves(held.pop("low"), y).astype(BF16))

        _lookahead(TILES_PER_ITER * HEADS, scores, finish)
        return carry

    lax.fori_loop(0, SEQ // (TQ * TILES_PER_ITER), tiles, 0)


def _attn_full(kernel, q, kt, v, extra_in=(), extra_specs=(), scalar=None, name=None):
    b = q.shape[0]
    per_batch = lambda a: pl.BlockSpec((None,) + a.shape[1:], lambda i: (i, 0, 0))
    in_specs = [per_batch(q), per_batch(kt), per_batch(v)]
    args = [q, kt, v]
    if scalar is not None:
        in_specs = [pl.BlockSpec(memory_space=pltpu.SMEM)] + in_specs
        args = [scalar] + args
    in_specs += list(extra_specs)
    args += list(extra_in)
    return pl.pallas_call(
        kernel,
        grid=(b,),
        in_specs=in_specs,
        out_specs=pl.BlockSpec((None, SEQ, BRANCH_W), lambda i: (i, 0, 0)),
        out_shape=jax.ShapeDtypeStruct((b, SEQ, BRANCH_W), BF16),
        compiler_params=pltpu.CompilerParams(
            dimension_semantics=("arbitrary",), vmem_limit_bytes=VMEM_LIMIT),
        name=name,
    )(*args)


def _attn_b_kernel(q_ref, k_ref, v_ref, bias_ref, o_ref):
    def steps(i, carry):
        def window(t):
            st = i * NB_STEPS_PER_ITER + t
            start = pl.multiple_of(jnp.clip(st - 2, 0, NB_STEPS - 5) * NB_Q, NB_Q)
            variant = jnp.minimum(st, 2) + jnp.maximum(st - (NB_STEPS - 3), 0)
            return pl.ds(pl.multiple_of(st * NB_Q, NB_Q), NB_Q), pl.ds(start, NB_KEYS), variant

        def scores(job):
            t, head = divmod(job, HEADS)
            rows, keys, variant = window(t)
            cols = slice(head * HEAD_DIM, (head + 1) * HEAD_DIM)
            s = lax.dot_general(q_ref[rows, cols], k_ref[keys, cols],
                                (((1,), (1,)), ((), ())), preferred_element_type=F32)
            return s + bias_ref[variant, head]

        held = {}

        def finish(job, s):
            t, head = divmod(job, HEADS)
            rows, keys, _ = window(t)
            pair, half = divmod(head, 2)
            y = _normalise_by_ones_lanes(
                jnp.dot(_exp2_terms(s).astype(BF16), v_ref[keys, head * LANES:(head + 1) * LANES],
                        preferred_element_type=F32))
            if half == 0:
                held["low"] = y
                return
            o_ref[rows, pair * LANES:(pair + 1) * LANES] = _join_halves(held.pop("low"), y).astype(BF16)

        _lookahead(NB_STEPS_PER_ITER * HEADS, scores, finish)
        return carry

    lax.fori_loop(0, NB_STEPS // NB_STEPS_PER_ITER, steps, 0)


def _attn_b(q, k, v, bias):
    b = q.shape[0]
    per_batch = lambda a: pl.BlockSpec((None,) + a.shape[1:], lambda i: (i, 0, 0))
    return pl.pallas_call(
        _attn_b_kernel,
        grid=(b,),
        in_specs=[per_batch(q), per_batch(k), per_batch(v),
                  pl.BlockSpec(bias.shape, lambda i: (0, 0, 0, 0), pipeline_mode=pl.Buffered(1))],
        out_specs=pl.BlockSpec((None, SEQ, BRANCH_W), lambda i: (i, 0, 0)),
        out_shape=jax.ShapeDtypeStruct((b, SEQ, BRANCH_W), BF16),
        compiler_params=pltpu.CompilerParams(
            dimension_semantics=("arbitrary",), vmem_limit_bytes=VMEM_LIMIT),
        name="attn_b",
    )(q, k, v, bias)


def _nb_bias_tables(rpb):
    rows = SEQ // GRID_W
    steps = np.array([0, 1, 2, NB_STEPS - 2, NB_STEPS - 1])
    start_row = np.clip(steps - 2, 0, NB_STEPS - 5) * 2
    qr = np.arange(2)
    kr = np.arange(NB_KEYS // GRID_W)
    r = 2 * steps[:, None, None] + qr[None, :, None]
    krow = start_row[:, None, None] + kr[None, None, :]
    rs = np.clip(r - WIN_R // 2, 0, rows - WIN_R)
    row_ok = (krow >= rs) & (krow < rs + WIN_R)
    dr = np.clip(krow - r + WIN_R - 1, 0, 2 * WIN_R - 2)
    qc = np.arange(GRID_W)[:, None]
    kc = np.arange(GRID_W)[None, :]
    cs = np.clip(qc - WIN_C // 2, 0, GRID_W - WIN_C)
    col_ok = (kc >= cs) & (kc < cs + WIN_C)
    dc = np.clip(kc - qc + WIN_C - 1, 0, 2 * WIN_C - 2)
    row_sel = np.eye(2 * WIN_R - 1, dtype=np.float32)[dr.reshape(-1)]
    col_sel = np.eye(2 * WIN_C - 1, dtype=np.float32)[dc.reshape(-1)].T
    hp = lax.Precision.HIGHEST
    t = jnp.einsum("ar,hrc->hac", row_sel, rpb.astype(F32), precision=hp)
    t = jnp.einsum("hac,ck->hak", t, col_sel, precision=hp)
    nkr = NB_KEYS // GRID_W
    t = t.reshape(HEADS, NB_VARIANTS, 2, nkr, GRID_W, GRID_W)
    t = jnp.transpose(t, (1, 0, 2, 4, 3, 5)).reshape(NB_VARIANTS, HEADS, NB_Q, NB_KEYS)
    valid = row_ok[:, :, None, :, None] & col_ok[None, None, :, None, :]
    valid = valid.reshape(NB_VARIANTS, 1, NB_Q, NB_KEYS)
    return jnp.where(jnp.asarray(valid), t * LOG2E, NEG)


def _merge_kernel(x_ref, g_ref, oa_ref, ob_ref, oc_ref, od_ref, wg_ref, wb_ref, wo_ref, y_ref):
    x = x_ref[...]
    h = _rms(x, g_ref[...]).astype(BF16)
    merged = None
    for i, o_ref in enumerate((oa_ref, ob_ref, oc_ref, od_ref)):
        gate = jax.nn.sigmoid(jnp.dot(h, wg_ref[:, i * D_MODEL:(i + 1) * D_MODEL], preferred_element_type=F32))
        term = gate * jnp.dot(o_ref[...], wb_ref[i], preferred_element_type=F32)
        merged = term if merged is None else merged + term
    y_ref[...] = x + jnp.dot(merged.astype(BF16), wo_ref[...], preferred_element_type=F32)


def _merge(x, g, oa, ob, oc, od, wg, wb, wo):
    t = x.shape[0]
    tm = TM_MERGE
    tok = lambda c: pl.BlockSpec((tm, c), lambda i: (i, 0))
    full = lambda a: pl.BlockSpec(a.shape, lambda i: (0,) * a.ndim)
    return pl.pallas_call(
        _merge_kernel,
        grid=(t // tm,),
        in_specs=[tok(D_MODEL), full(g), tok(BRANCH_W), tok(BRANCH_W), tok(BRANCH_W), tok(BRANCH_W),
                  full(wg), full(wb), full(wo)],
        out_specs=tok(D_MODEL),
        out_shape=jax.ShapeDtypeStruct((t, D_MODEL), F32),
        compiler_params=pltpu.CompilerParams(
            dimension_semantics=("arbitrary",), vmem_limit_bytes=VMEM_LIMIT),
        name="merge",
    )(x, g, oa, ob, oc, od, wg, wb, wo)


def _ffn_kernel(xp_ref, x_ref, xn_ref, g_ref, win_ref, cw_ref, cb_ref, wo_ref, fg_ref, y_ref, act_scr, *, final):
    j = pl.program_id(1)
    tm = x_ref.shape[0]
    ext = tm + 2 * SUBLANES
    x = x_ref[...]
    h = _rms(jnp.concatenate([xp_ref[...], x, xn_ref[...]], axis=0), g_ref[...]).astype(BF16)
    keep_prev = (j > 0).astype(F32)
    keep_next = (j < pl.num_programs(1) - 1).astype(F32)
    row = lax.broadcasted_iota(jnp.int32, (ext, 1), 0)
    halo_scale = jnp.where(row < SUBLANES, keep_prev, jnp.where(row >= tm + SUBLANES, keep_next, 1.0))
    for c in range(N_FFN_CHUNKS):
        cols = slice(c * FFN_CHUNK, (c + 1) * FFN_CHUNK)
        u = jnp.dot(h, win_ref[:, cols], preferred_element_type=F32)
        g = jnp.dot(h, win_ref[:, FFN_DIM + c * FFN_CHUNK:FFN_DIM + (c + 1) * FFN_CHUNK],
                    preferred_element_type=F32) * halo_scale
        gc = (pltpu.roll(g, 1, 0) * cw_ref[0:1, cols] + g * cw_ref[1:2, cols]
              + pltpu.roll(g, ext - 1, 0) * cw_ref[2:3, cols] + cb_ref[:, cols])
        act = jax.nn.silu(gc) * u
        act_scr[:, cols] = act[SUBLANES:SUBLANES + tm].astype(BF16)
    y = x + jnp.dot(act_scr[...], wo_ref[...], preferred_element_type=F32)
    if final:
        y = _rms(y, fg_ref[...])
    y_ref[...] = y


def _ffn(x, g, w_in, conv_w, conv_b, w_out, final_g, final):
    b = x.shape[0]
    tm = TM_FFN
    per = tm // SUBLANES
    last = SEQ // SUBLANES - 1
    const = lambda a: pl.BlockSpec(a.shape, lambda i, j: (0,) * a.ndim, pipeline_mode=pl.Buffered(1))
    return pl.pallas_call(
        functools.partial(_ffn_kernel, final=final),
        grid=(b, SEQ // tm),
        in_specs=[pl.BlockSpec((None, SUBLANES, D_MODEL), lambda i, j: (i, jnp.maximum(j * per - 1, 0), 0)),
                  pl.BlockSpec((None, tm, D_MODEL), lambda i, j: (i, j, 0)),
                  pl.BlockSpec((None, SUBLANES, D_MODEL), lambda i, j: (i, jnp.minimum((j + 1) * per, last), 0)),
                  const(g), const(w_in), const(conv_w), const(conv_b), const(w_out), const(final_g)],
        out_specs=pl.BlockSpec((None, tm, D_MODEL), lambda i, j: (i, j, 0)),
        out_shape=jax.ShapeDtypeStruct((b, SEQ, D_MODEL), F32),
        scratch_shapes=[pltpu.VMEM((tm, FFN_DIM), BF16)],
        compiler_params=pltpu.CompilerParams(
            dimension_semantics=("arbitrary", "arbitrary"), vmem_limit_bytes=VMEM_LIMIT),
        name="ffn",
    )(x, x, x, g, w_in, conv_w, conv_b, w_out, final_g)


def _rope_lane_tables():
    t = jnp.arange(SEQ)
    inv_freq = ROPE_THETA ** (-jnp.arange(0, 2 * ROPE_HALF, 2, dtype=F32) / (2 * ROPE_HALF))
    ang = lambda pos: pos.astype(F32)[:, None] * inv_freq[None, :]
    lane = np.arange(LANES)
    freq = lane % ROPE_HALF
    first = jnp.asarray((lane % (2 * ROPE_HALF)) < ROPE_HALF)[None, :]
    ang_seq = ang(t)[:, freq]
    ang_axial = jnp.where(jnp.asarray((lane // (2 * ROPE_HALF)) % 2 == 0)[None, :],
                          ang(t // GRID_W)[:, freq], ang(t % GRID_W)[:, freq])
    active = jnp.asarray((lane >= C_NOPE) & (lane < C_NOPE + C_ROPE))[None, :]

    def trio(a, on):
        cos, sin = jnp.cos(a), jnp.sin(a)
        return [jnp.where(on, cos, 1.0), jnp.where(on & first, -sin, 0.0), jnp.where(on & ~first, sin, 0.0)]

    everywhere = jnp.ones((1, LANES), bool)
    return jnp.stack(trio(ang_seq, everywhere) + trio(ang_seq, active) + trio(ang_axial, everywhere))


def _pack_layer(l, w_in, c_q_norm, c_kv_norm, c_w_uq, c_w_ukv, d_q_norm, d_k_norm):
    w = w_in[l]
    a, bb, cc, dd, gate = jnp.split(w, [768, 1536, 1888, 2400], axis=1)
    z = lambda n: jnp.zeros((D_MODEL, n), F32)
    cq, ckv, krope = cc[:, :C_QLORA], cc[:, C_QLORA:C_QLORA + C_KVLORA], cc[:, C_QLORA + C_KVLORA:]
    w_mix = jnp.concatenate([a, bb, dd, cq, z(64), ckv, z(C_NOPE), krope, z(LANES - C_NOPE - C_ROPE)], axis=1)
    assert w_mix.shape[1] == MIX_COLS
    wuq = c_w_uq[l].reshape(C_QLORA, HEADS, C_NOPE + C_ROPE)
    wuq = jnp.pad(wuq, ((0, 256 - C_QLORA), (0, 0), (0, LANES - C_NOPE - C_ROPE))).reshape(256, HEADS * LANES)
    wukv = c_w_ukv[l].reshape(C_KVLORA, HEADS, C_NOPE + HEAD_DIM)
    halves = ((0, 0), (0, 0), (0, LANES - HEAD_DIM))
    wk = jnp.pad(wukv[:, :, :C_NOPE], halves).reshape(C_KVLORA, HEADS * LANES)
    wv = jnp.pad(wukv[:, :, C_NOPE:], halves).reshape(C_KVLORA, HEADS * LANES)
    wukv = jnp.concatenate([wk, wv], axis=1)
    cqn = jnp.pad(c_q_norm[l], (0, 256 - C_QLORA))[None, :]
    return dict(
        w_mix=w_mix.astype(BF16), w_gate=gate.astype(BF16), wuq=wuq.astype(BF16), wukv=wukv.astype(BF16),
        cqn=cqn, ckvn=c_kv_norm[l][None, :],
        dqn=jnp.tile(d_q_norm[l], 2)[None, :], dkn=jnp.tile(d_k_norm[l], 2)[None, :])


def _trunk(x, tables, layers, final_norm):
    b = x.shape[0]
    for l, p in enumerate(layers):
        (qa, kat, va, qb, kb, vb, qc, kct, vc, qd, kdt, vd) = _inproj(
            x, p["attn_norm"], p["w_mix"], tables, p["cqn"], p["ckvn"], p["wuq"], p["wukv"], p["dqn"], p["dkn"])
        lam_init = 0.8 - 0.6 * math.exp(-0.3 * l)
        oa = _attn_full(functools.partial(_attn_a_kernel, out_scale=1.0 - lam_init), qa, kat, va,
                        extra_in=(p["subln"],), extra_specs=(pl.BlockSpec((1, LANES), lambda i: (0, 0)),),
                        scalar=p["lam"], name="attn_a")
        ob = _attn_b(qb, kb, vb, p["nb_bias"])
        oc = _attn_full(_attn_c_kernel, qc, kct, vc, name="attn_c")
        od = _attn_full(_attn_d_kernel, qd, kdt, vd, name="attn_d")
        flat = lambda a: a.reshape(b * SEQ, a.shape[-1])
        x1 = _merge(flat(x), p["attn_norm"], flat(oa), flat(ob), flat(oc), flat(od),
                    p["w_gate"], p["w_branch"], p["w_out"]).reshape(b, SEQ, D_MODEL)
        x = _ffn(x1, p["ffn_norm"], p["w_ffn_in"], p["conv_w"], p["conv_b"], p["w_ffn_out"],
                 final_norm, final=(l == DEPTH - 1))
    return x


def kernel(x_prompt, x_sample, attn_norm, w_in, a_lambda_q1, a_lambda_k1, a_lambda_q2, a_lambda_k2, a_subln, b_rpb, c_q_norm, c_kv_norm, c_w_uq, c_w_ukv, d_q_norm, d_k_norm, w_branch, w_out, ffn_norm, w_ffn_in, ffn_conv_w, ffn_conv_b, w_ffn_out, final_norm):
    tables = _rope_lane_tables()
    layers = []
    for l in range(DEPTH):
        p = _pack_layer(l, w_in, c_q_norm, c_kv_norm, c_w_uq, c_w_ukv, d_q_norm, d_k_norm)
        lam_init = 0.8 - 0.6 * math.exp(-0.3 * l)
        lam = (jnp.exp(jnp.sum(a_lambda_q1[l] * a_lambda_k1[l]))
               - jnp.exp(jnp.sum(a_lambda_q2[l] * a_lambda_k2[l])) + lam_init)
        p.update(
            attn_norm=attn_norm[l][None, :], lam=lam.reshape(1).astype(F32),
            subln=jnp.tile(a_subln[l], 2)[None, :], nb_bias=_nb_bias_tables(b_rpb[l]),
            w_branch=w_branch[l].astype(BF16), w_out=w_out[l].astype(BF16),
            ffn_norm=ffn_norm[l][None, :], w_ffn_in=w_ffn_in[l].astype(BF16),
            conv_w=ffn_conv_w[l], conv_b=ffn_conv_b[l][None, :], w_ffn_out=w_ffn_out[l].astype(BF16))
        layers.append(p)
    fn = final_norm[None, :]
    return (_trunk(x_prompt, tables, layers, fn), _trunk(x_sample, tables, layers, fn))
```

```python
import functools
import math

import numpy as np
import jax
import jax.numpy as jnp
from jax import lax
from jax.experimental import pallas as pl
from jax.experimental.pallas import tpu as pltpu

F32 = jnp.float32
BF16 = jnp.bfloat16

D_MODEL = 1024
SEQ = 2048
DEPTH = 2
GRID_W = 64
ROPE_THETA = 10000.0
EPS = 1e-6
N_BRANCH = 4
HEADS = 4
HEAD_DIM = 64
A_D = 32
WIN_R = 8
WIN_C = 16
C_NOPE = 64
C_ROPE = 32
C_QLORA = 192
C_KVLORA = 128
D_KV_HEADS = 2
BRANCH_W = 256
FFN_DIM = 2816
LANES = 128
SUBLANES = 8
ROPE_HALF = 16
LOG2E = 1.4426950408889634

MIX_COLS = 2560
TM_IN = 512
TM_MERGE = 512
TQ = 256
TILES_PER_ITER = 2
NB_STEPS_PER_ITER = 2
NB_Q = 2 * GRID_W
NB_KEYS = 10 * GRID_W
NB_STEPS = SEQ // NB_Q
NB_VARIANTS = 5
NEG = -1e30
TM_FFN = 512
FFN_CHUNK = 256
N_FFN_CHUNKS = FFN_DIM // FFN_CHUNK
VMEM_LIMIT = 56 * 1024 * 1024


def _rms(x, g):
    return x * lax.rsqrt(jnp.mean(x * x, axis=-1, keepdims=True) + EPS) * g


def _rope_slab(x, cos, sin_lo, sin_hi):
    return (x * cos + pltpu.roll(x, LANES - ROPE_HALF, 1) * sin_lo
            + pltpu.roll(x, ROPE_HALF, 1) * sin_hi)


def _group_mean_sq(x, width):
    sq = x * x
    hi = sq.astype(BF16)
    lo = (sq - hi.astype(F32)).astype(BF16)
    r = lax.broadcasted_iota(jnp.int32, (LANES, LANES), 0) // width
    c = lax.broadcasted_iota(jnp.int32, (LANES, LANES), 1) // width
    ones = jnp.where(r == c, 1.0, 0.0).astype(BF16)
    tot = (jnp.dot(hi, ones, preferred_element_type=F32)
           + jnp.dot(lo, ones, preferred_element_type=F32))
    return tot * (1.0 / width)


def _low_half(rows):
    return lax.broadcasted_iota(jnp.int32, (rows, LANES), 1) < HEAD_DIM


def _value_with_ones(v, shift):
    if shift:
        v = pltpu.roll(v, HEAD_DIM, 1)
    return jnp.where(_low_half(v.shape[0]), v, 1.0).astype(BF16)


def _inproj_kernel(x_ref, g_ref, w_ref, tab_ref, cqn_ref, ckvn_ref, wuq_ref, wukv_ref, dqn_ref, dkn_ref,
                   qa_ref, kat_ref, va_ref, qb_ref, kb_ref, vb_ref,
                   qc_ref, kct_ref, vc_ref, qd_ref, kdt_ref, vd_ref):
    h = _rms(x_ref[...], g_ref[...]).astype(BF16)

    def project(first_slab, n_slabs):
        cols = slice(first_slab * LANES, (first_slab + n_slabs) * LANES)
        p = jnp.dot(h, w_ref[:, cols], preferred_element_type=F32)
        return lambda i: p[:, (i - first_slab) * LANES:(i - first_slab + 1) * LANES]

    def rope(x, variant):
        return _rope_slab(x, tab_ref[3 * variant], tab_ref[3 * variant + 1], tab_ref[3 * variant + 2])

    def store_slabs(ref, slabs):
        for i, v in enumerate(slabs):
            ref[:, i * LANES:(i + 1) * LANES] = v.astype(BF16)

    c_slab = project(16, 4)
    a_slab = project(0, 6)
    cq = jnp.concatenate([c_slab(16), c_slab(17)], axis=1)
    cq = cq * lax.rsqrt(jnp.sum(cq * cq, axis=-1, keepdims=True) * (1.0 / C_QLORA) + EPS) * cqn_ref[...]
    q = jnp.dot(cq.astype(BF16), wuq_ref[...], preferred_element_type=F32)
    ckv = _rms(c_slab(18), ckvn_ref[...])
    kv = jnp.dot(ckv.astype(BF16), wukv_ref[...], preferred_element_type=F32)
    b_slab = project(6, 6)
    store_slabs(qa_ref, [rope(a_slab(i), 0) * (A_D ** -0.5 * LOG2E) for i in range(2)])
    for i in range(2):
        kat_ref[i * LANES:(i + 1) * LANES, :] = rope(a_slab(2 + i), 0).T.astype(BF16)
    store_slabs(va_ref, [a_slab(4), a_slab(5)])
    d_slab = project(12, 4)
    store_slabs(qb_ref, [b_slab(6 + i) * (HEAD_DIM ** -0.5 * LOG2E) for i in range(2)])
    store_slabs(kb_ref, [b_slab(8), b_slab(9)])
    store_slabs(vb_ref, [_value_with_ones(b_slab(10 + i // 2), i % 2) for i in range(HEADS)])
    kr = rope(c_slab(19), 1)
    store_slabs(qc_ref, [rope(q[:, i * LANES:(i + 1) * LANES], 1) * ((C_NOPE + C_ROPE) ** -0.5 * LOG2E)
                         for i in range(HEADS)])
    for i in range(HEADS):
        kct_ref[i * LANES:(i + 1) * LANES, :] = (kv[:, i * LANES:(i + 1) * LANES] + kr).T.astype(BF16)
    store_slabs(vc_ref, [_value_with_ones(kv[:, (HEADS + i) * LANES:(HEADS + i + 1) * LANES], 0)
                         for i in range(HEADS)])
    qd = []
    for i in range(2):
        x = d_slab(12 + i)
        x = x * lax.rsqrt(_group_mean_sq(x, HEAD_DIM) + EPS) * dqn_ref[...]
        qd.append(rope(x, 2) * (HEAD_DIM ** -0.5 * LOG2E))
    store_slabs(qd_ref, qd)
    x = d_slab(14)
    x = x * lax.rsqrt(_group_mean_sq(x, HEAD_DIM) + EPS) * dkn_ref[...]
    kdt_ref[...] = rope(x, 2).T.astype(BF16)
    store_slabs(vd_ref, [_value_with_ones(d_slab(15), i) for i in range(D_KV_HEADS)])


def _inproj(x, g, w_mix, tables, cqn, ckvn, wuq, wukv, dqn, dkn):
    b = x.shape[0]
    tm = TM_IN
    grid = (SEQ // tm, b)
    tok = lambda c: pl.BlockSpec((None, tm, c), lambda j, i: (i, j, 0))
    tr = lambda c: pl.BlockSpec((None, c, tm), lambda j, i: (i, 0, j))
    full = lambda a: pl.BlockSpec(a.shape, lambda j, i: (0,) * a.ndim)
    out_cols = [("tok", 256), ("tr", 256), ("tok", 256), ("tok", 256), ("tok", 256), ("tok", 512),
                ("tok", 512), ("tr", 512), ("tok", 512), ("tok", 256), ("tr", 128), ("tok", 256)]
    out_specs = [tok(c) if k == "tok" else tr(c) for k, c in out_cols]
    out_shape = [jax.ShapeDtypeStruct((b, SEQ, c) if k == "tok" else (b, c, SEQ), BF16) for k, c in out_cols]
    return pl.pallas_call(
        _inproj_kernel,
        grid=grid,
        in_specs=[tok(D_MODEL), full(g), full(w_mix),
                  pl.BlockSpec((9, tm, LANES), lambda j, i: (0, j, 0)),
                  full(cqn), full(ckvn), full(wuq), full(wukv), full(dqn), full(dkn)],
        out_specs=out_specs,
        out_shape=out_shape,
        compiler_params=pltpu.CompilerParams(
            dimension_semantics=("arbitrary", "arbitrary"), vmem_limit_bytes=VMEM_LIMIT),
        name="inproj",
    )(x, g, w_mix, tables, cqn, ckvn, wuq, wukv, dqn, dkn)


def _exp2_terms(s):
    return jnp.exp2(s - jnp.max(s, axis=-1, keepdims=True))


def _softmax_weights(job, s):
    return _exp2_terms(s).astype(BF16)


def _normalise_by_ones_lanes(o):
    return o * (1.0 / pltpu.roll(o, HEAD_DIM, 1))


def _join_halves(y_low, y_high):
    return jnp.where(_low_half(y_low.shape[0]), y_low, pltpu.roll(y_high, HEAD_DIM, 1))


def _pipeline(n, scores, weights, output):
    s = {0: scores(0)}
    w = {}
    for j in range(n + 1):
        if j + 1 < n:
            s[j + 1] = scores(j + 1)
        if j >= 1:
            output(j - 1, w.pop(j - 1))
        if j < n:
            w[j] = weights(j, s.pop(j))


def _tile_rows(i, t):
    return pl.ds(pl.multiple_of((i * TILES_PER_ITER + t) * TQ, TQ), TQ)


def _attn_a_kernel(lam_ref, q_ref, kt_ref, v_ref, g_ref, o_ref, *, out_scale):
    lam = lam_ref[0]

    def tiles(i, carry):
        def scores(job):
            t, head = divmod(job, HEADS)
            return tuple(
                jnp.dot(q_ref[_tile_rows(i, t), A_D * j:A_D * (j + 1)], kt_ref[A_D * j:A_D * (j + 1), :],
                        preferred_element_type=F32) for j in (2 * head, 2 * head + 1))

        def weights(job, s):
            e0, e1 = _exp2_terms(s[0]), _exp2_terms(s[1])
            l0 = jnp.sum(e0, axis=-1, keepdims=True)
            l1 = jnp.sum(e1, axis=-1, keepdims=True)
            return (e0 - e1 * (lam * l0 / l1)).astype(BF16), l0

        held = {}

        def output(job, wl):
            t, head = divmod(job, HEADS)
            pair, half = divmod(head, 2)
            w, l0 = wl
            o = jnp.dot(w, v_ref[:, pair * LANES:(pair + 1) * LANES], preferred_element_type=F32) * (1.0 / l0)
            mask = _low_half(TQ) if half == 0 else ~_low_half(TQ)
            ms = jnp.sum(jnp.where(mask, o * o, 0.0), axis=-1, keepdims=True) * (1.0 / HEAD_DIM)
            y = o * lax.rsqrt(ms + EPS)
            if half == 0:
                held["low"] = y
                return
            y = jnp.where(_low_half(TQ), held.pop("low"), y)
            o_ref[_tile_rows(i, t), pair * LANES:(pair + 1) * LANES] = (y * g_ref[...] * out_scale).astype(BF16)

        _pipeline(TILES_PER_ITER * HEADS, scores, weights, output)
        return carry

    lax.fori_loop(0, SEQ // (TQ * TILES_PER_ITER), tiles, 0)


def _attn_c_kernel(q_ref, kt_ref, v_ref, o_ref):
    def tiles(i, carry):
        def scores(job):
            t, head = divmod(job, HEADS)
            cols = slice(head * LANES, (head + 1) * LANES)
            return jnp.dot(q_ref[_tile_rows(i, t), cols], kt_ref[cols, :], preferred_element_type=F32)

        held = {}

        def output(job, p):
            t, head = divmod(job, HEADS)
            pair, half = divmod(head, 2)
            y = _normalise_by_ones_lanes(
                jnp.dot(p, v_ref[:, head * LANES:(head + 1) * LANES], preferred_element_type=F32))
            if half == 0:
                held["low"] = y
                return
            o_ref[_tile_rows(i, t), pair * LANES:(pair + 1) * LANES] = _join_halves(held.pop("low"), y).astype(BF16)

        _pipeline(TILES_PER_ITER * HEADS, scores, _softmax_weights, output)
        return carry

    lax.fori_loop(0, SEQ // (TQ * TILES_PER_ITER), tiles, 0)


def _attn_d_kernel(q_ref, kt_ref, v_ref, o_ref):
    def tiles(i, carry):
        def scores(job):
            t, head = divmod(job, HEADS)
            kv_head = head // 2
            return jnp.dot(q_ref[_tile_rows(i, t), head * HEAD_DIM:(head + 1) * HEAD_DIM],
                           kt_ref[kv_head * HEAD_DIM:(kv_head + 1) * HEAD_DIM, :], preferred_element_type=F32)

        held = {}

        def output(job, p):
            t, head = divmod(job, HEADS)
            kv_head, g = divmod(head, 2)
            y = _normalise_by_ones_lanes(
                jnp.dot(p, v_ref[:, kv_head * LANES:(kv_head + 1) * LANES], preferred_element_type=F32))
            if g == 0:
                held["low"] = y
                return
            o_ref[_tile_rows(i, t), kv_head * LANES:(kv_head + 1) * LANES] = (
                _join_halves(held.pop("low"), y).astype(BF16))

        _pipeline(TILES_PER_ITER * HEADS, scores, _softmax_weights, output)
        return carry

    lax.fori_loop(0, SEQ // (TQ * TILES_PER_ITER), tiles, 0)


def _attn_full(kernel, q, kt, v, extra_in=(), extra_specs=(), scalar=None, name=None):
    b = q.shape[0]
    per_batch = lambda a: pl.BlockSpec((None,) + a.shape[1:], lambda i: (i, 0, 0))
    in_specs = [per_batch(q), per_batch(kt), per_batch(v)]
    args = [q, kt, v]
    if scalar is not None:
        in_specs = [pl.BlockSpec(memory_space=pltpu.SMEM)] + in_specs
        args = [scalar] + args
    in_specs += list(extra_specs)
    args += list(extra_in)
    return pl.pallas_call(
        kernel,
        grid=(b,),
        in_specs=in_specs,
        out_specs=pl.BlockSpec((None, SEQ, BRANCH_W), lambda i: (i, 0, 0)),
        out_shape=jax.ShapeDtypeStruct((b, SEQ, BRANCH_W), BF16),
        compiler_params=pltpu.CompilerParams(
            dimension_semantics=("arbitrary",), vmem_limit_bytes=VMEM_LIMIT),
        name=name,
    )(*args)


def _attn_b_kernel(q_ref, k_ref, v_ref, bias_ref, o_ref):
    def steps(i, carry):
        def window(t):
            st = i * NB_STEPS_PER_ITER + t
            start = pl.multiple_of(jnp.clip(st - 2, 0, NB_STEPS - 5) * NB_Q, NB_Q)
            variant = jnp.minimum(st, 2) + jnp.maximum(st - (NB_STEPS - 3), 0)
            return pl.ds(pl.multiple_of(st * NB_Q, NB_Q), NB_Q), pl.ds(start, NB_KEYS), variant

        def scores(job):
            t, head = divmod(job, HEADS)
            rows, keys, variant = window(t)
            cols = slice(head * HEAD_DIM, (head + 1) * HEAD_DIM)
            s = lax.dot_general(q_ref[rows, cols], k_ref[keys, cols],
                                (((1,), (1,)), ((), ())), preferred_element_type=F32)
            return s + bias_ref[variant, head]

        held = {}

        def output(job, p):
            t, head = divmod(job, HEADS)
            rows, keys, _ = window(t)
            pair, half = divmod(head, 2)
            y = _normalise_by_ones_lanes(
                jnp.dot(p, v_ref[keys, head * LANES:(head + 1) * LANES], preferred_element_type=F32))
            if half == 0:
                held["low"] = y
                return
            o_ref[rows, pair * LANES:(pair + 1) * LANES] = _join_halves(held.pop("low"), y).astype(BF16)

        _pipeline(NB_STEPS_PER_ITER * HEADS, scores, _softmax_weights, output)
        return carry

    lax.fori_loop(0, NB_STEPS // NB_STEPS_PER_ITER, steps, 0)


def _attn_b(q, k, v, bias):
    b = q.shape[0]
    per_batch = lambda a: pl.BlockSpec((None,) + a.shape[1:], lambda i: (i, 0, 0))
    return pl.pallas_call(
        _attn_b_kernel,
        grid=(b,),
        in_specs=[per_batch(q), per_batch(k), per_batch(v),
                  pl.BlockSpec(bias.shape, lambda i: (0, 0, 0, 0), pipeline_mode=pl.Buffered(1))],
        out_specs=pl.BlockSpec((None, SEQ, BRANCH_W), lambda i: (i, 0, 0)),
        out_shape=jax.ShapeDtypeStruct((b, SEQ, BRANCH_W), BF16),
        compiler_params=pltpu.CompilerParams(
            dimension_semantics=("arbitrary",), vmem_limit_bytes=VMEM_LIMIT),
        name="attn_b",
    )(q, k, v, bias)


def _nb_bias_tables(rpb):
    rows = SEQ // GRID_W
    steps = np.array([0, 1, 2, NB_STEPS - 2, NB_STEPS - 1])
    start_row = np.clip(steps - 2, 0, NB_STEPS - 5) * 2
    qr = np.arange(2)
    kr = np.arange(NB_KEYS // GRID_W)
    r = 2 * steps[:, None, None] + qr[None, :, None]
    krow = start_row[:, None, None] + kr[None, None, :]
    rs = np.clip(r - WIN_R // 2, 0, rows - WIN_R)
    row_ok = (krow >= rs) & (krow < rs + WIN_R)
    dr = np.clip(krow - r + WIN_R - 1, 0, 2 * WIN_R - 2)
    qc = np.arange(GRID_W)[:, None]
    kc = np.arange(GRID_W)[None, :]
    cs = np.clip(qc - WIN_C // 2, 0, GRID_W - WIN_C)
    col_ok = (kc >= cs) & (kc < cs + WIN_C)
    dc = np.clip(kc - qc + WIN_C - 1, 0, 2 * WIN_C - 2)
    row_sel = np.eye(2 * WIN_R - 1, dtype=np.float32)[dr.reshape(-1)]
    col_sel = np.eye(2 * WIN_C - 1, dtype=np.float32)[dc.reshape(-1)].T
    hp = lax.Precision.HIGHEST
    t = jnp.einsum("ar,hrc->hac", row_sel, rpb.astype(F32), precision=hp)
    t = jnp.einsum("hac,ck->hak", t, col_sel, precision=hp)
    nkr = NB_KEYS // GRID_W
    t = t.reshape(HEADS, NB_VARIANTS, 2, nkr, GRID_W, GRID_W)
    t = jnp.transpose(t, (1, 0, 2, 4, 3, 5)).reshape(NB_VARIANTS, HEADS, NB_Q, NB_KEYS)
    valid = row_ok[:, :, None, :, None] & col_ok[None, None, :, None, :]
    valid = valid.reshape(NB_VARIANTS, 1, NB_Q, NB_KEYS)
    return jnp.where(jnp.asarray(valid), t * LOG2E, NEG)


def _merge_kernel(x_ref, g_ref, oa_ref, ob_ref, oc_ref, od_ref, wg_ref, wb_ref, wo_ref, y_ref):
    x = x_ref[...]
    h = _rms(x, g_ref[...]).astype(BF16)
    merged = None
    for i, o_ref in enumerate((oa_ref, ob_ref, oc_ref, od_ref)):
        gate = jax.nn.sigmoid(jnp.dot(h, wg_ref[:, i * D_MODEL:(i + 1) * D_MODEL], preferred_element_type=F32))
        term = gate * jnp.dot(o_ref[...], wb_ref[i], preferred_element_type=F32)
        merged = term if merged is None else merged + term
    y_ref[...] = x + jnp.dot(merged.astype(BF16), wo_ref[...], preferred_element_type=F32)


def _merge(x, g, oa, ob, oc, od, wg, wb, wo):
    t = x.shape[0]
    tm = TM_MERGE
    tok = lambda c: pl.BlockSpec((tm, c), lambda i: (i, 0))
    full = lambda a: pl.BlockSpec(a.shape, lambda i: (0,) * a.ndim)
    return pl.pallas_call(
        _merge_kernel,
        grid=(t // tm,),
        in_specs=[tok(D_MODEL), full(g), tok(BRANCH_W), tok(BRANCH_W), tok(BRANCH_W), tok(BRANCH_W),
                  full(wg), full(wb), full(wo)],
        out_specs=tok(D_MODEL),
        out_shape=jax.ShapeDtypeStruct((t, D_MODEL), F32),
        compiler_params=pltpu.CompilerParams(
            dimension_semantics=("arbitrary",), vmem_limit_bytes=VMEM_LIMIT),
        name="merge",
    )(x, g, oa, ob, oc, od, wg, wb, wo)


def _ffn_kernel(xp_ref, x_ref, xn_ref, g_ref, win_ref, cw_ref, cb_ref, wo_ref, fg_ref, y_ref, act_scr, *, final):
    j = pl.program_id(1)
    tm = x_ref.shape[0]
    ext = tm + 2 * SUBLANES
    x = x_ref[...]
    h = _rms(jnp.concatenate([xp_ref[...], x, xn_ref[...]], axis=0), g_ref[...]).astype(BF16)
    keep_prev = (j > 0).astype(F32)
    keep_next = (j < pl.num_programs(1) - 1).astype(F32)
    row = lax.broadcasted_iota(jnp.int32, (ext, 1), 0)
    halo_scale = jnp.where(row < SUBLANES, keep_prev, jnp.where(row >= tm + SUBLANES, keep_next, 1.0))
    for c in range(N_FFN_CHUNKS):
        cols = slice(c * FFN_CHUNK, (c + 1) * FFN_CHUNK)
        u = jnp.dot(h, win_ref[:, cols], preferred_element_type=F32)
        g = jnp.dot(h, win_ref[:, FFN_DIM + c * FFN_CHUNK:FFN_DIM + (c + 1) * FFN_CHUNK],
                    preferred_element_type=F32) * halo_scale
        gc = (pltpu.roll(g, 1, 0) * cw_ref[0:1, cols] + g * cw_ref[1:2, cols]
              + pltpu.roll(g, ext - 1, 0) * cw_ref[2:3, cols] + cb_ref[:, cols])
        act = jax.nn.silu(gc) * u
        act_scr[:, cols] = act[SUBLANES:SUBLANES + tm].astype(BF16)
    y = x + jnp.dot(act_scr[...], wo_ref[...], preferred_element_type=F32)
    if final:
        y = _rms(y, fg_ref[...])
    y_ref[...] = y


def _ffn(x, g, w_in, conv_w, conv_b, w_out, final_g, final):
    b = x.shape[0]
    tm = TM_FFN
    per = tm // SUBLANES
    last = SEQ // SUBLANES - 1
    const = lambda a: pl.BlockSpec(a.shape, lambda i, j: (0,) * a.ndim, pipeline_mode=pl.Buffered(1))
    return pl.pallas_call(
        functools.partial(_ffn_kernel, final=final),
        grid=(b, SEQ // tm),
        in_specs=[pl.BlockSpec((None, SUBLANES, D_MODEL), lambda i, j: (i, jnp.maximum(j * per - 1, 0), 0)),
                  pl.BlockSpec((None, tm, D_MODEL), lambda i, j: (i, j, 0)),
                  pl.BlockSpec((None, SUBLANES, D_MODEL), lambda i, j: (i, jnp.minimum((j + 1) * per, last), 0)),
                  const(g), const(w_in), const(conv_w), const(conv_b), const(w_out), const(final_g)],
        out_specs=pl.BlockSpec((None, tm, D_MODEL), lambda i, j: (i, j, 0)),
        out_shape=jax.ShapeDtypeStruct((b, SEQ, D_MODEL), F32),
        scratch_shapes=[pltpu.VMEM((tm, FFN_DIM), BF16)],
        compiler_params=pltpu.CompilerParams(
            dimension_semantics=("arbitrary", "arbitrary"), vmem_limit_bytes=VMEM_LIMIT),
        name="ffn",
    )(x, x, x, g, w_in, conv_w, conv_b, w_out, final_g)


def _rope_lane_tables():
    t = jnp.arange(SEQ)
    inv_freq = ROPE_THETA ** (-jnp.arange(0, 2 * ROPE_HALF, 2, dtype=F32) / (2 * ROPE_HALF))
    ang = lambda pos: pos.astype(F32)[:, None] * inv_freq[None, :]
    lane = np.arange(LANES)
    freq = lane % ROPE_HALF
    first = jnp.asarray((lane % (2 * ROPE_HALF)) < ROPE_HALF)[None, :]
    ang_seq = ang(t)[:, freq]
    ang_axial = jnp.where(jnp.asarray((lane // (2 * ROPE_HALF)) % 2 == 0)[None, :],
                          ang(t // GRID_W)[:, freq], ang(t % GRID_W)[:, freq])
    active = jnp.asarray((lane >= C_NOPE) & (lane < C_NOPE + C_ROPE))[None, :]

    def trio(a, on):
        cos, sin = jnp.cos(a), jnp.sin(a)
        return [jnp.where(on, cos, 1.0), jnp.where(on & first, -sin, 0.0), jnp.where(on & ~first, sin, 0.0)]

    everywhere = jnp.ones((1, LANES), bool)
    return jnp.stack(trio(ang_seq, everywhere) + trio(ang_seq, active) + trio(ang_axial, everywhere))


def _pack_layer(l, w_in, c_q_norm, c_kv_norm, c_w_uq, c_w_ukv, d_q_norm, d_k_norm):
    w = w_in[l]
    a, bb, cc, dd, gate = jnp.split(w, [768, 1536, 1888, 2400], axis=1)
    z = lambda n: jnp.zeros((D_MODEL, n), F32)
    cq, ckv, krope = cc[:, :C_QLORA], cc[:, C_QLORA:C_QLORA + C_KVLORA], cc[:, C_QLORA + C_KVLORA:]
    w_mix = jnp.concatenate([a, bb, dd, cq, z(64), ckv, z(C_NOPE), krope, z(LANES - C_NOPE - C_ROPE)], axis=1)
    assert w_mix.shape[1] == MIX_COLS
    wuq = c_w_uq[l].reshape(C_QLORA, HEADS, C_NOPE + C_ROPE)
    wuq = jnp.pad(wuq, ((0, 256 - C_QLORA), (0, 0), (0, LANES - C_NOPE - C_ROPE))).reshape(256, HEADS * LANES)
    wukv = c_w_ukv[l].reshape(C_KVLORA, HEADS, C_NOPE + HEAD_DIM)
    halves = ((0, 0), (0, 0), (0, LANES - HEAD_DIM))
    wk = jnp.pad(wukv[:, :, :C_NOPE], halves).reshape(C_KVLORA, HEADS * LANES)
    wv = jnp.pad(wukv[:, :, C_NOPE:], halves).reshape(C_KVLORA, HEADS * LANES)
    wukv = jnp.concatenate([wk, wv], axis=1)
    cqn = jnp.pad(c_q_norm[l], (0, 256 - C_QLORA))[None, :]
    return dict(
        w_mix=w_mix.astype(BF16), w_gate=gate.astype(BF16), wuq=wuq.astype(BF16), wukv=wukv.astype(BF16),
        cqn=cqn, ckvn=c_kv_norm[l][None, :],
        dqn=jnp.tile(d_q_norm[l], 2)[None, :], dkn=jnp.tile(d_k_norm[l], 2)[None, :])


def _trunk(x, tables, layers, final_norm):
    b = x.shape[0]
    for l, p in enumerate(layers):
        (qa, kat, va, qb, kb, vb, qc, kct, vc, qd, kdt, vd) = _inproj(
            x, p["attn_norm"], p["w_mix"], tables, p["cqn"], p["ckvn"], p["wuq"], p["wukv"], p["dqn"], p["dkn"])
        lam_init = 0.8 - 0.6 * math.exp(-0.3 * l)
        oa = _attn_full(functools.partial(_attn_a_kernel, out_scale=1.0 - lam_init), qa, kat, va,
                        extra_in=(p["subln"],), extra_specs=(pl.BlockSpec((1, LANES), lambda i: (0, 0)),),
                        scalar=p["lam"], name="attn_a")
        ob = _attn_b(qb, kb, vb, p["nb_bias"])
        oc = _attn_full(_attn_c_kernel, qc, kct, vc, name="attn_c")
        od = _attn_full(_attn_d_kernel, qd, kdt, vd, name="attn_d")
        flat = lambda a: a.reshape(b * SEQ, a.shape[-1])
        x1 = _merge(flat(x), p["attn_norm"], flat(oa), flat(ob), flat(oc), flat(od),
                    p["w_gate"], p["w_branch"], p["w_out"]).reshape(b, SEQ, D_MODEL)
        x = _ffn(x1, p["ffn_norm"], p["w_ffn_in"], p["conv_w"], p["conv_b"], p["w_ffn_out"],
                 final_norm, final=(l == DEPTH - 1))
    return x


def kernel(x_prompt, x_sample, attn_norm, w_in, a_lambda_q1, a_lambda_k1, a_lambda_q2, a_lambda_k2, a_subln, b_rpb, c_q_norm, c_kv_norm, c_w_uq, c_w_ukv, d_q_norm, d_k_norm, w_branch, w_out, ffn_norm, w_ffn_in, ffn_conv_w, ffn_conv_b, w_ffn_out, final_norm):
    tables = _rope_lane_tables()
    layers = []
    for l in range(DEPTH):
        p = _pack_layer(l, w_in, c_q_norm, c_kv_norm, c_w_uq, c_w_ukv, d_q_norm, d_k_norm)
        lam_init = 0.8 - 0.6 * math.exp(-0.3 * l)
        lam = (jnp.exp(jnp.sum(a_lambda_q1[l] * a_lambda_k1[l]))
               - jnp.exp(jnp.sum(a_lambda_q2[l] * a_lambda_k2[l])) + lam_init)
        p.update(
            attn_norm=attn_norm[l][None, :], lam=lam.reshape(1).astype(F32),
            subln=jnp.tile(a_subln[l], 2)[None, :], nb_bias=_nb_bias_tables(b_rpb[l]),
            w_branch=w_branch[l].astype(BF16), w_out=w_out[l].astype(BF16),
            ffn_norm=ffn_norm[l][None, :], w_ffn_in=w_ffn_in[l].astype(BF16),
            conv_w=ffn_conv_w[l], conv_b=ffn_conv_b[l][None, :], w_ffn_out=w_ffn_out[l].astype(BF16))
        layers.append(p)
    fn = final_norm[None, :]
    return (_trunk(x_prompt, tables, layers, fn), _trunk(x_sample, tables, layers, fn))
```

```python
import functools
import math

import numpy as np
import jax
import jax.numpy as jnp
from jax import lax
from jax.experimental import pallas as pl
from jax.experimental.pallas import tpu as pltpu

F32 = jnp.float32
BF16 = jnp.bfloat16

D_MODEL = 1024
SEQ = 2048
DEPTH = 2
GRID_W = 64
ROPE_THETA = 10000.0
EPS = 1e-6
N_BRANCH = 4
HEADS = 4
HEAD_DIM = 64
A_D = 32
WIN_R = 8
WIN_C = 16
C_NOPE = 64
C_ROPE = 32
C_QLORA = 192
C_KVLORA = 128
D_KV_HEADS = 2
BRANCH_W = 256
FFN_DIM = 2816
LANES = 128
SUBLANES = 8
ROPE_HALF = 16
LOG2E = 1.4426950408889634

MIX_COLS = 2560
TM_IN = 512
TM_MERGE = 512
TQ = 256
TILES_PER_ITER = 2
NB_STEPS_PER_ITER = 8
NB_Q = 2 * GRID_W
NB_KEYS = 10 * GRID_W
NB_STEPS = SEQ // NB_Q
NB_VARIANTS = 5
NEG = -1e30
TM_FFN = 512
FFN_CHUNK = 256
N_FFN_CHUNKS = FFN_DIM // FFN_CHUNK
VMEM_LIMIT = 56 * 1024 * 1024


def _rms(x, g):
    return x * lax.rsqrt(jnp.mean(x * x, axis=-1, keepdims=True) + EPS) * g


def _rope_slab(x, cos, sin_lo, sin_hi):
    return (x * cos + pltpu.roll(x, LANES - ROPE_HALF, 1) * sin_lo
            + pltpu.roll(x, ROPE_HALF, 1) * sin_hi)


def _group_mean_sq(x, width):
    sq = x * x
    hi = sq.astype(BF16)
    lo = (sq - hi.astype(F32)).astype(BF16)
    r = lax.broadcasted_iota(jnp.int32, (LANES, LANES), 0) // width
    c = lax.broadcasted_iota(jnp.int32, (LANES, LANES), 1) // width
    ones = jnp.where(r == c, 1.0, 0.0).astype(BF16)
    tot = (jnp.dot(hi, ones, preferred_element_type=F32)
           + jnp.dot(lo, ones, preferred_element_type=F32))
    return tot * (1.0 / width)


def _low_half(rows):
    return lax.broadcasted_iota(jnp.int32, (rows, LANES), 1) < HEAD_DIM


def _value_with_ones(v, shift):
    if shift:
        v = pltpu.roll(v, HEAD_DIM, 1)
    return jnp.where(_low_half(v.shape[0]), v, 1.0).astype(BF16)


def _inproj_kernel(x_ref, g_ref, w_ref, tab_ref, cqn_ref, ckvn_ref, wuq_ref, wukv_ref, dqn_ref, dkn_ref,
                   qa_ref, kat_ref, va_ref, qb_ref, kbt_ref, vb_ref,
                   qc_ref, kct_ref, vc_ref, qd_ref, kdt_ref, vd_ref):
    h = _rms(x_ref[...], g_ref[...]).astype(BF16)

    def project(first_slab, n_slabs):
        cols = slice(first_slab * LANES, (first_slab + n_slabs) * LANES)
        p = jnp.dot(h, w_ref[:, cols], preferred_element_type=F32)
        return lambda i: p[:, (i - first_slab) * LANES:(i - first_slab + 1) * LANES]

    def rope(x, variant):
        return _rope_slab(x, tab_ref[3 * variant], tab_ref[3 * variant + 1], tab_ref[3 * variant + 2])

    def store_slabs(ref, slabs):
        for i, v in enumerate(slabs):
            ref[:, i * LANES:(i + 1) * LANES] = v.astype(BF16)

    c_slab = project(16, 4)
    a_slab = project(0, 6)
    cq = jnp.concatenate([c_slab(16), c_slab(17)], axis=1)
    cq = cq * lax.rsqrt(jnp.sum(cq * cq, axis=-1, keepdims=True) * (1.0 / C_QLORA) + EPS) * cqn_ref[...]
    q = jnp.dot(cq.astype(BF16), wuq_ref[...], preferred_element_type=F32)
    ckv = _rms(c_slab(18), ckvn_ref[...])
    kv = jnp.dot(ckv.astype(BF16), wukv_ref[...], preferred_element_type=F32)
    b_slab = project(6, 6)
    store_slabs(qa_ref, [rope(a_slab(i), 0) * (A_D ** -0.5 * LOG2E) for i in range(2)])
    for i in range(2):
        kat_ref[i * LANES:(i + 1) * LANES, :] = rope(a_slab(2 + i), 0).T.astype(BF16)
    store_slabs(va_ref, [a_slab(4), a_slab(5)])
    d_slab = project(12, 4)
    store_slabs(qb_ref, [b_slab(6 + i) * (HEAD_DIM ** -0.5 * LOG2E) for i in range(2)])
    for i in range(2):
        kbt_ref[i * LANES:(i + 1) * LANES, :] = b_slab(8 + i).T.astype(BF16)
    store_slabs(vb_ref, [_value_with_ones(b_slab(10 + i // 2), i % 2) for i in range(HEADS)])
    kr = rope(c_slab(19), 1)
    store_slabs(qc_ref, [rope(q[:, i * LANES:(i + 1) * LANES], 1) * ((C_NOPE + C_ROPE) ** -0.5 * LOG2E)
                         for i in range(HEADS)])
    for i in range(HEADS):
        kct_ref[i * LANES:(i + 1) * LANES, :] = (kv[:, i * LANES:(i + 1) * LANES] + kr).T.astype(BF16)
    store_slabs(vc_ref, [_value_with_ones(kv[:, (HEADS + i) * LANES:(HEADS + i + 1) * LANES], 0)
                         for i in range(HEADS)])
    qd = []
    for i in range(2):
        x = d_slab(12 + i)
        x = x * lax.rsqrt(_group_mean_sq(x, HEAD_DIM) + EPS) * dqn_ref[...]
        qd.append(rope(x, 2) * (HEAD_DIM ** -0.5 * LOG2E))
    store_slabs(qd_ref, qd)
    x = d_slab(14)
    x = x * lax.rsqrt(_group_mean_sq(x, HEAD_DIM) + EPS) * dkn_ref[...]
    kdt_ref[...] = rope(x, 2).T.astype(BF16)
    store_slabs(vd_ref, [_value_with_ones(d_slab(15), i) for i in range(D_KV_HEADS)])


def _inproj(x, g, w_mix, tables, cqn, ckvn, wuq, wukv, dqn, dkn):
    b = x.shape[0]
    tm = TM_IN
    grid = (SEQ // tm, b)
    tok = lambda c: pl.BlockSpec((None, tm, c), lambda j, i: (i, j, 0))
    tr = lambda c: pl.BlockSpec((None, c, tm), lambda j, i: (i, 0, j))
    full = lambda a: pl.BlockSpec(a.shape, lambda j, i: (0,) * a.ndim)
    out_cols = [("tok", 256), ("tr", 256), ("tok", 256), ("tok", 256), ("tr", 256), ("tok", 512),
                ("tok", 512), ("tr", 512), ("tok", 512), ("tok", 256), ("tr", 128), ("tok", 256)]
    out_specs = [tok(c) if k == "tok" else tr(c) for k, c in out_cols]
    out_shape = [jax.ShapeDtypeStruct((b, SEQ, c) if k == "tok" else (b, c, SEQ), BF16) for k, c in out_cols]
    return pl.pallas_call(
        _inproj_kernel,
        grid=grid,
        in_specs=[tok(D_MODEL), full(g), full(w_mix),
                  pl.BlockSpec((9, tm, LANES), lambda j, i: (0, j, 0)),
                  full(cqn), full(ckvn), full(wuq), full(wukv), full(dqn), full(dkn)],
        out_specs=out_specs,
        out_shape=out_shape,
        compiler_params=pltpu.CompilerParams(
            dimension_semantics=("arbitrary", "arbitrary"), vmem_limit_bytes=VMEM_LIMIT),
        name="inproj",
    )(x, g, w_mix, tables, cqn, ckvn, wuq, wukv, dqn, dkn)


def _exp2_terms(s):
    return jnp.exp2(s - jnp.max(s, axis=-1, keepdims=True))


def _softmax_weights(job, s):
    return _exp2_terms(s).astype(BF16)


def _normalise_by_ones_lanes(o):
    return o * (1.0 / pltpu.roll(o, HEAD_DIM, 1))


def _join_halves(y_low, y_high):
    return jnp.where(_low_half(y_low.shape[0]), y_low, pltpu.roll(y_high, HEAD_DIM, 1))


def _pipeline(n, scores, weights, output):
    s = {0: scores(0)}
    w = {}
    for j in range(n + 1):
        if j + 1 < n:
            s[j + 1] = scores(j + 1)
        if j >= 1:
            output(j - 1, w.pop(j - 1))
        if j < n:
            w[j] = weights(j, s.pop(j))


def _tile_rows(i, t):
    return pl.ds(pl.multiple_of((i * TILES_PER_ITER + t) * TQ, TQ), TQ)


def _attn_a_kernel(lam_ref, q_ref, kt_ref, v_ref, g_ref, o_ref, *, out_scale):
    lam = lam_ref[0]

    def tiles(i, carry):
        def scores(job):
            t, head = divmod(job, HEADS)
            return tuple(
                jnp.dot(q_ref[_tile_rows(i, t), A_D * j:A_D * (j + 1)], kt_ref[A_D * j:A_D * (j + 1), :],
                        preferred_element_type=F32) for j in (2 * head, 2 * head + 1))

        def weights(job, s):
            e0, e1 = _exp2_terms(s[0]), _exp2_terms(s[1])
            l0 = jnp.sum(e0, axis=-1, keepdims=True)
            l1 = jnp.sum(e1, axis=-1, keepdims=True)
            return (e0 - e1 * (lam * l0 / l1)).astype(BF16), l0

        held = {}

        def output(job, wl):
            t, head = divmod(job, HEADS)
            pair, half = divmod(head, 2)
            w, l0 = wl
            o = jnp.dot(w, v_ref[:, pair * LANES:(pair + 1) * LANES], preferred_element_type=F32) * (1.0 / l0)
            mask = _low_half(TQ) if half == 0 else ~_low_half(TQ)
            ms = jnp.sum(jnp.where(mask, o * o, 0.0), axis=-1, keepdims=True) * (1.0 / HEAD_DIM)
            y = o * lax.rsqrt(ms + EPS)
            if half == 0:
                held["low"] = y
                return
            y = jnp.where(_low_half(TQ), held.pop("low"), y)
            o_ref[_tile_rows(i, t), pair * LANES:(pair + 1) * LANES] = (y * g_ref[...] * out_scale).astype(BF16)

        _pipeline(TILES_PER_ITER * HEADS, scores, weights, output)
        return carry

    lax.fori_loop(0, SEQ // (TQ * TILES_PER_ITER), tiles, 0)


def _attn_c_kernel(q_ref, kt_ref, v_ref, o_ref):
    def tiles(i, carry):
        def scores(job):
            t, head = divmod(job, HEADS)
            cols = slice(head * LANES, (head + 1) * LANES)
            return jnp.dot(q_ref[_tile_rows(i, t), cols], kt_ref[cols, :], preferred_element_type=F32)

        held = {}

        def output(job, p):
            t, head = divmod(job, HEADS)
            pair, half = divmod(head, 2)
            y = _normalise_by_ones_lanes(
                jnp.dot(p, v_ref[:, head * LANES:(head + 1) * LANES], preferred_element_type=F32))
            if half == 0:
                held["low"] = y
                return
            o_ref[_tile_rows(i, t), pair * LANES:(pair + 1) * LANES] = _join_halves(held.pop("low"), y).astype(BF16)

        _pipeline(TILES_PER_ITER * HEADS, scores, _softmax_weights, output)
        return carry

    lax.fori_loop(0, SEQ // (TQ * TILES_PER_ITER), tiles, 0)


def _attn_d_kernel(q_ref, kt_ref, v_ref, o_ref):
    def tiles(i, carry):
        def scores(job):
            t, head = divmod(job, HEADS)
            kv_head = head // 2
            return jnp.dot(q_ref[_tile_rows(i, t), head * HEAD_DIM:(head + 1) * HEAD_DIM],
                           kt_ref[kv_head * HEAD_DIM:(kv_head + 1) * HEAD_DIM, :], preferred_element_type=F32)

        held = {}

        def output(job, p):
            t, head = divmod(job, HEADS)
            kv_head, g = divmod(head, 2)
            y = _normalise_by_ones_lanes(
                jnp.dot(p, v_ref[:, kv_head * LANES:(kv_head + 1) * LANES], preferred_element_type=F32))
            if g == 0:
                held["low"] = y
                return
            o_ref[_tile_rows(i, t), kv_head * LANES:(kv_head + 1) * LANES] = (
                _join_halves(held.pop("low"), y).astype(BF16))

        _pipeline(TILES_PER_ITER * HEADS, scores, _softmax_weights, output)
        return carry

    lax.fori_loop(0, SEQ // (TQ * TILES_PER_ITER), tiles, 0)


def _attn_full(kernel, q, kt, v, extra_in=(), extra_specs=(), scalar=None, name=None):
    b = q.shape[0]
    per_batch = lambda a: pl.BlockSpec((None,) + a.shape[1:], lambda i: (i, 0, 0))
    in_specs = [per_batch(q), per_batch(kt), per_batch(v)]
    args = [q, kt, v]
    if scalar is not None:
        in_specs = [pl.BlockSpec(memory_space=pltpu.SMEM)] + in_specs
        args = [scalar] + args
    in_specs += list(extra_specs)
    args += list(extra_in)
    return pl.pallas_call(
        kernel,
        grid=(b,),
        in_specs=in_specs,
        out_specs=pl.BlockSpec((None, SEQ, BRANCH_W), lambda i: (i, 0, 0)),
        out_shape=jax.ShapeDtypeStruct((b, SEQ, BRANCH_W), BF16),
        compiler_params=pltpu.CompilerParams(
            dimension_semantics=("arbitrary",), vmem_limit_bytes=VMEM_LIMIT),
        name=name,
    )(*args)


def _attn_b_kernel(q_ref, kt_ref, v_ref, bias_ref, o_ref):
    def steps(i, carry):
        def window(t):
            st = i * NB_STEPS_PER_ITER + t
            start = pl.multiple_of(jnp.clip(st - 2, 0, NB_STEPS - 5) * NB_Q, NB_Q)
            variant = jnp.minimum(st, 2) + jnp.maximum(st - (NB_STEPS - 3), 0)
            return pl.ds(pl.multiple_of(st * NB_Q, NB_Q), NB_Q), pl.ds(start, NB_KEYS), variant

        def scores(job):
            t, head = divmod(job, HEADS)
            rows, keys, variant = window(t)
            cols = slice(head * HEAD_DIM, (head + 1) * HEAD_DIM)
            s = jnp.dot(q_ref[rows, cols], kt_ref[cols, keys], preferred_element_type=F32)
            return s + bias_ref[variant, head]

        held = {}

        def output(job, p):
            t, head = divmod(job, HEADS)
            rows, keys, _ = window(t)
            pair, half = divmod(head, 2)
            y = _normalise_by_ones_lanes(
                jnp.dot(p, v_ref[keys, head * LANES:(head + 1) * LANES], preferred_element_type=F32))
            if half == 0:
                held["low"] = y
                return
            o_ref[rows, pair * LANES:(pair + 1) * LANES] = _join_halves(held.pop("low"), y).astype(BF16)

        _pipeline(NB_STEPS_PER_ITER * HEADS, scores, _softmax_weights, output)
        return carry

    lax.fori_loop(0, NB_STEPS // NB_STEPS_PER_ITER, steps, 0)


def _attn_b(q, k, v, bias):
    b = q.shape[0]
    per_batch = lambda a: pl.BlockSpec((None,) + a.shape[1:], lambda i: (i, 0, 0))
    return pl.pallas_call(
        _attn_b_kernel,
        grid=(b,),
        in_specs=[per_batch(q), per_batch(k), per_batch(v),
                  pl.BlockSpec(bias.shape, lambda i: (0, 0, 0, 0), pipeline_mode=pl.Buffered(1))],
        out_specs=pl.BlockSpec((None, SEQ, BRANCH_W), lambda i: (i, 0, 0)),
        out_shape=jax.ShapeDtypeStruct((b, SEQ, BRANCH_W), BF16),
        compiler_params=pltpu.CompilerParams(
            dimension_semantics=("arbitrary",), vmem_limit_bytes=VMEM_LIMIT),
        name="attn_b",
    )(q, k, v, bias)


def _nb_bias_tables(rpb):
    rows = SEQ // GRID_W
    steps = np.array([0, 1, 2, NB_STEPS - 2, NB_STEPS - 1])
    start_row = np.clip(steps - 2, 0, NB_STEPS - 5) * 2
    qr = np.arange(2)
    kr = np.arange(NB_KEYS // GRID_W)
    r = 2 * steps[:, None, None] + qr[None, :, None]
    krow = start_row[:, None, None] + kr[None, None, :]
    rs = np.clip(r - WIN_R // 2, 0, rows - WIN_R)
    row_ok = (krow >= rs) & (krow < rs + WIN_R)
    dr = np.clip(krow - r + WIN_R - 1, 0, 2 * WIN_R - 2)
    qc = np.arange(GRID_W)[:, None]
    kc = np.arange(GRID_W)[None, :]
    cs = np.clip(qc - WIN_C // 2, 0, GRID_W - WIN_C)
    col_ok = (kc >= cs) & (kc < cs + WIN_C)
    dc = np.clip(kc - qc + WIN_C - 1, 0, 2 * WIN_C - 2)
    row_sel = np.eye(2 * WIN_R - 1, dtype=np.float32)[dr.reshape(-1)]
    col_sel = np.eye(2 * WIN_C - 1, dtype=np.float32)[dc.reshape(-1)].T
    hp = lax.Precision.HIGHEST
    t = jnp.einsum("ar,hrc->hac", row_sel, rpb.astype(F32), precision=hp)
    t = jnp.einsum("hac,ck->hak", t, col_sel, precision=hp)
    nkr = NB_KEYS // GRID_W
    t = t.reshape(HEADS, NB_VARIANTS, 2, nkr, GRID_W, GRID_W)
    t = jnp.transpose(t, (1, 0, 2, 4, 3, 5)).reshape(NB_VARIANTS, HEADS, NB_Q, NB_KEYS)
    valid = row_ok[:, :, None, :, None] & col_ok[None, None, :, None, :]
    valid = valid.reshape(NB_VARIANTS, 1, NB_Q, NB_KEYS)
    return jnp.where(jnp.asarray(valid), t * LOG2E, NEG)


def _merge_kernel(x_ref, g_ref, oa_ref, ob_ref, oc_ref, od_ref, wg_ref, wb_ref, wo_ref, y_ref):
    x = x_ref[...]
    h = _rms(x, g_ref[...]).astype(BF16)
    merged = None
    for i, o_ref in enumerate((oa_ref, ob_ref, oc_ref, od_ref)):
        gate = jax.nn.sigmoid(jnp.dot(h, wg_ref[:, i * D_MODEL:(i + 1) * D_MODEL], preferred_element_type=F32))
        term = gate * jnp.dot(o_ref[...], wb_ref[i], preferred_element_type=F32)
        merged = term if merged is None else merged + term
    y_ref[...] = x + jnp.dot(merged.astype(BF16), wo_ref[...], preferred_element_type=F32)


def _merge(x, g, oa, ob, oc, od, wg, wb, wo):
    t = x.shape[0]
    tm = TM_MERGE
    tok = lambda c: pl.BlockSpec((tm, c), lambda i: (i, 0))
    full = lambda a: pl.BlockSpec(a.shape, lambda i: (0,) * a.ndim)
    return pl.pallas_call(
        _merge_kernel,
        grid=(t // tm,),
        in_specs=[tok(D_MODEL), full(g), tok(BRANCH_W), tok(BRANCH_W), tok(BRANCH_W), tok(BRANCH_W),
                  full(wg), full(wb), full(wo)],
        out_specs=tok(D_MODEL),
        out_shape=jax.ShapeDtypeStruct((t, D_MODEL), F32),
        compiler_params=pltpu.CompilerParams(
            dimension_semantics=("arbitrary",), vmem_limit_bytes=VMEM_LIMIT),
        name="merge",
    )(x, g, oa, ob, oc, od, wg, wb, wo)


def _ffn_kernel(xp_ref, x_ref, xn_ref, g_ref, win_ref, cw_ref, cb_ref, wo_ref, fg_ref, y_ref, act_scr, *, final):
    j = pl.program_id(1)
    tm = x_ref.shape[0]
    ext = tm + 2 * SUBLANES
    x = x_ref[...]
    h = _rms(jnp.concatenate([xp_ref[...], x, xn_ref[...]], axis=0), g_ref[...]).astype(BF16)
    keep_prev = (j > 0).astype(F32)
    keep_next = (j < pl.num_programs(1) - 1).astype(F32)
    row = lax.broadcasted_iota(jnp.int32, (ext, 1), 0)
    halo_scale = jnp.where(row < SUBLANES, keep_prev, jnp.where(row >= tm + SUBLANES, keep_next, 1.0))
    for c in range(N_FFN_CHUNKS):
        cols = slice(c * FFN_CHUNK, (c + 1) * FFN_CHUNK)
        u = jnp.dot(h, win_ref[:, cols], preferred_element_type=F32)
        g = jnp.dot(h, win_ref[:, FFN_DIM + c * FFN_CHUNK:FFN_DIM + (c + 1) * FFN_CHUNK],
                    preferred_element_type=F32) * halo_scale
        gc = (pltpu.roll(g, 1, 0) * cw_ref[0:1, cols] + g * cw_ref[1:2, cols]
              + pltpu.roll(g, ext - 1, 0) * cw_ref[2:3, cols] + cb_ref[:, cols])
        act = jax.nn.silu(gc) * u
        act_scr[:, cols] = act[SUBLANES:SUBLANES + tm].astype(BF16)
    y = x + jnp.dot(act_scr[...], wo_ref[...], preferred_element_type=F32)
    if final:
        y = _rms(y, fg_ref[...])
    y_ref[...] = y


def _ffn(x, g, w_in, conv_w, conv_b, w_out, final_g, final):
    b = x.shape[0]
    tm = TM_FFN
    per = tm // SUBLANES
    last = SEQ // SUBLANES - 1
    const = lambda a: pl.BlockSpec(a.shape, lambda i, j: (0,) * a.ndim, pipeline_mode=pl.Buffered(1))
    return pl.pallas_call(
        functools.partial(_ffn_kernel, final=final),
        grid=(b, SEQ // tm),
        in_specs=[pl.BlockSpec((None, SUBLANES, D_MODEL), lambda i, j: (i, jnp.maximum(j * per - 1, 0), 0)),
                  pl.BlockSpec((None, tm, D_MODEL), lambda i, j: (i, j, 0)),
                  pl.BlockSpec((None, SUBLANES, D_MODEL), lambda i, j: (i, jnp.minimum((j + 1) * per, last), 0)),
                  const(g), const(w_in), const(conv_w), const(conv_b), const(w_out), const(final_g)],
        out_specs=pl.BlockSpec((None, tm, D_MODEL), lambda i, j: (i, j, 0)),
        out_shape=jax.ShapeDtypeStruct((b, SEQ, D_MODEL), F32),
        scratch_shapes=[pltpu.VMEM((tm, FFN_DIM), BF16)],
        compiler_params=pltpu.CompilerParams(
            dimension_semantics=("arbitrary", "arbitrary"), vmem_limit_bytes=VMEM_LIMIT),
        name="ffn",
    )(x, x, x, g, w_in, conv_w, conv_b, w_out, final_g)


def _rope_lane_tables():
    t = jnp.arange(SEQ)
    inv_freq = ROPE_THETA ** (-jnp.arange(0, 2 * ROPE_HALF, 2, dtype=F32) / (2 * ROPE_HALF))
    ang = lambda pos: pos.astype(F32)[:, None] * inv_freq[None, :]
    lane = np.arange(LANES)
    freq = lane % ROPE_HALF
    first = jnp.asarray((lane % (2 * ROPE_HALF)) < ROPE_HALF)[None, :]
    ang_seq = ang(t)[:, freq]
    ang_axial = jnp.where(jnp.asarray((lane // (2 * ROPE_HALF)) % 2 == 0)[None, :],
                          ang(t // GRID_W)[:, freq], ang(t % GRID_W)[:, freq])
    active = jnp.asarray((lane >= C_NOPE) & (lane < C_NOPE + C_ROPE))[None, :]

    def trio(a, on):
        cos, sin = jnp.cos(a), jnp.sin(a)
        return [jnp.where(on, cos, 1.0), jnp.where(on & first, -sin, 0.0), jnp.where(on & ~first, sin, 0.0)]

    everywhere = jnp.ones((1, LANES), bool)
    return jnp.stack(trio(ang_seq, everywhere) + trio(ang_seq, active) + trio(ang_axial, everywhere))


def _pack_layer(l, w_in, c_q_norm, c_kv_norm, c_w_uq, c_w_ukv, d_q_norm, d_k_norm):
    w = w_in[l]
    a, bb, cc, dd, gate = jnp.split(w, [768, 1536, 1888, 2400], axis=1)
    z = lambda n: jnp.zeros((D_MODEL, n), F32)
    cq, ckv, krope = cc[:, :C_QLORA], cc[:, C_QLORA:C_QLORA + C_KVLORA], cc[:, C_QLORA + C_KVLORA:]
    w_mix = jnp.concatenate([a, bb, dd, cq, z(64), ckv, z(C_NOPE), krope, z(LANES - C_NOPE - C_ROPE)], axis=1)
    assert w_mix.shape[1] == MIX_COLS
    wuq = c_w_uq[l].reshape(C_QLORA, HEADS, C_NOPE + C_ROPE)
    wuq = jnp.pad(wuq, ((0, 256 - C_QLORA), (0, 0), (0, LANES - C_NOPE - C_ROPE))).reshape(256, HEADS * LANES)
    wukv = c_w_ukv[l].reshape(C_KVLORA, HEADS, C_NOPE + HEAD_DIM)
    halves = ((0, 0), (0, 0), (0, LANES - HEAD_DIM))
    wk = jnp.pad(wukv[:, :, :C_NOPE], halves).reshape(C_KVLORA, HEADS * LANES)
    wv = jnp.pad(wukv[:, :, C_NOPE:], halves).reshape(C_KVLORA, HEADS * LANES)
    wukv = jnp.concatenate([wk, wv], axis=1)
    cqn = jnp.pad(c_q_norm[l], (0, 256 - C_QLORA))[None, :]
    return dict(
        w_mix=w_mix.astype(BF16), w_gate=gate.astype(BF16), wuq=wuq.astype(BF16), wukv=wukv.astype(BF16),
        cqn=cqn, ckvn=c_kv_norm[l][None, :],
        dqn=jnp.tile(d_q_norm[l], 2)[None, :], dkn=jnp.tile(d_k_norm[l], 2)[None, :])


def _trunk(x, tables, layers, final_norm):
    b = x.shape[0]
    for l, p in enumerate(layers):
        (qa, kat, va, qb, kb, vb, qc, kct, vc, qd, kdt, vd) = _inproj(
            x, p["attn_norm"], p["w_mix"], tables, p["cqn"], p["ckvn"], p["wuq"], p["wukv"], p["dqn"], p["dkn"])
        lam_init = 0.8 - 0.6 * math.exp(-0.3 * l)
        oa = _attn_full(functools.partial(_attn_a_kernel, out_scale=1.0 - lam_init), qa, kat, va,
                        extra_in=(p["subln"],), extra_specs=(pl.BlockSpec((1, LANES), lambda i: (0, 0)),),
                        scalar=p["lam"], name="attn_a")
        ob = _attn_b(qb, kb, vb, p["nb_bias"])
        oc = _attn_full(_attn_c_kernel, qc, kct, vc, name="attn_c")
        od = _attn_full(_attn_d_kernel, qd, kdt, vd, name="attn_d")
        flat = lambda a: a.reshape(b * SEQ, a.shape[-1])
        x1 = _merge(flat(x), p["attn_norm"], flat(oa), flat(ob), flat(oc), flat(od),
                    p["w_gate"], p["w_branch"], p["w_out"]).reshape(b, SEQ, D_MODEL)
        x = _ffn(x1, p["ffn_norm"], p["w_ffn_in"], p["conv_w"], p["conv_b"], p["w_ffn_out"],
                 final_norm, final=(l == DEPTH - 1))
    return x


def kernel(x_prompt, x_sample, attn_norm, w_in, a_lambda_q1, a_lambda_k1, a_lambda_q2, a_lambda_k2, a_subln, b_rpb, c_q_norm, c_kv_norm, c_w_uq, c_w_ukv, d_q_norm, d_k_norm, w_branch, w_out, ffn_norm, w_ffn_in, ffn_conv_w, ffn_conv_b, w_ffn_out, final_norm):
    tables = _rope_lane_tables()
    layers = []
    for l in range(DEPTH):
        p = _pack_layer(l, w_in, c_q_norm, c_kv_norm, c_w_uq, c_w_ukv, d_q_norm, d_k_norm)
        lam_init = 0.8 - 0.6 * math.exp(-0.3 * l)
        lam = (jnp.exp(jnp.sum(a_lambda_q1[l] * a_lambda_k1[l]))
               - jnp.exp(jnp.sum(a_lambda_q2[l] * a_lambda_k2[l])) + lam_init)
        p.update(
            attn_norm=attn_norm[l][None, :], lam=lam.reshape(1).astype(F32),
            subln=jnp.tile(a_subln[l], 2)[None, :], nb_bias=_nb_bias_tables(b_rpb[l]),
            w_branch=w_branch[l].astype(BF16), w_out=w_out[l].astype(BF16),
            ffn_norm=ffn_norm[l][None, :], w_ffn_in=w_ffn_in[l].astype(BF16),
            conv_w=ffn_conv_w[l], conv_b=ffn_conv_b[l][None, :], w_ffn_out=w_ffn_out[l].astype(BF16))
        layers.append(p)
    fn = final_norm[None, :]
    return (_trunk(x_prompt, tables, layers, fn), _trunk(x_sample, tables, layers, fn))
```

```python
import functools
import math

import numpy as np
import jax
import jax.numpy as jnp
from jax import lax
from jax.experimental import pallas as pl
from jax.experimental.pallas import tpu as pltpu

F32 = jnp.float32
BF16 = jnp.bfloat16

D_MODEL = 1024
SEQ = 2048
DEPTH = 2
GRID_W = 64
ROPE_THETA = 10000.0
EPS = 1e-6
N_BRANCH = 4
HEADS = 4
HEAD_DIM = 64
A_D = 32
WIN_R = 8
WIN_C = 16
C_NOPE = 64
C_ROPE = 32
C_QLORA = 192
C_KVLORA = 128
D_KV_HEADS = 2
BRANCH_W = 256
FFN_DIM = 2816
LANES = 128
SUBLANES = 8
ROPE_HALF = 16
LOG2E = 1.4426950408889634

MIX_COLS = 2560
TM_IN = 512
TM_MERGE = 512
TQ = 256
TILES_PER_ITER = 1
NB_STEPS_PER_ITER = 8
NB_Q = 2 * GRID_W
NB_KEYS = 10 * GRID_W
NB_STEPS = SEQ // NB_Q
NB_VARIANTS = 5
NEG = -1e30
TM_FFN = 512
FFN_CHUNK = 256
N_FFN_CHUNKS = FFN_DIM // FFN_CHUNK
VMEM_LIMIT = 56 * 1024 * 1024


def _rms(x, g):
    return x * lax.rsqrt(jnp.mean(x * x, axis=-1, keepdims=True) + EPS) * g


def _rope_slab(x, cos, sin_lo, sin_hi):
    return (x * cos + pltpu.roll(x, LANES - ROPE_HALF, 1) * sin_lo
            + pltpu.roll(x, ROPE_HALF, 1) * sin_hi)


def _group_mean_sq(x, width):
    sq = x * x
    hi = sq.astype(BF16)
    lo = (sq - hi.astype(F32)).astype(BF16)
    r = lax.broadcasted_iota(jnp.int32, (LANES, LANES), 0) // width
    c = lax.broadcasted_iota(jnp.int32, (LANES, LANES), 1) // width
    ones = jnp.where(r == c, 1.0, 0.0).astype(BF16)
    tot = (jnp.dot(hi, ones, preferred_element_type=F32)
           + jnp.dot(lo, ones, preferred_element_type=F32))
    return tot * (1.0 / width)


def _low_half(rows):
    return lax.broadcasted_iota(jnp.int32, (rows, LANES), 1) < HEAD_DIM


def _value_with_ones(v, shift):
    if shift:
        v = pltpu.roll(v, HEAD_DIM, 1)
    return jnp.where(_low_half(v.shape[0]), v, 1.0).astype(BF16)


def _inproj_kernel(x_ref, g_ref, w_ref, tab_ref, cqn_ref, ckvn_ref, wuq_ref, wukv_ref, dqn_ref, dkn_ref,
                   qa_ref, kat_ref, va_ref, qb_ref, kbt_ref, vb_ref,
                   qc_ref, kct_ref, vc_ref, qd_ref, kdt_ref, vd_ref):
    h = _rms(x_ref[...], g_ref[...]).astype(BF16)

    def project(first_slab, n_slabs):
        cols = slice(first_slab * LANES, (first_slab + n_slabs) * LANES)
        p = jnp.dot(h, w_ref[:, cols], preferred_element_type=F32)
        return lambda i: p[:, (i - first_slab) * LANES:(i - first_slab + 1) * LANES]

    def rope(x, variant):
        return _rope_slab(x, tab_ref[3 * variant], tab_ref[3 * variant + 1], tab_ref[3 * variant + 2])

    def store_slabs(ref, slabs):
        for i, v in enumerate(slabs):
            ref[:, i * LANES:(i + 1) * LANES] = v.astype(BF16)

    c_slab = project(16, 4)
    a_slab = project(0, 6)
    cq = jnp.concatenate([c_slab(16), c_slab(17)], axis=1)
    cq = cq * lax.rsqrt(jnp.sum(cq * cq, axis=-1, keepdims=True) * (1.0 / C_QLORA) + EPS) * cqn_ref[...]
    q = jnp.dot(cq.astype(BF16), wuq_ref[...], preferred_element_type=F32)
    ckv = _rms(c_slab(18), ckvn_ref[...])
    kv = jnp.dot(ckv.astype(BF16), wukv_ref[...], preferred_element_type=F32)
    b_slab = project(6, 6)
    store_slabs(qa_ref, [rope(a_slab(i), 0) * (A_D ** -0.5 * LOG2E) for i in range(2)])
    for i in range(2):
        kat_ref[i * LANES:(i + 1) * LANES, :] = rope(a_slab(2 + i), 0).T.astype(BF16)
    store_slabs(va_ref, [a_slab(4), a_slab(5)])
    d_slab = project(12, 4)
    store_slabs(qb_ref, [b_slab(6 + i) * (HEAD_DIM ** -0.5 * LOG2E) for i in range(2)])
    for i in range(2):
        kbt_ref[i * LANES:(i + 1) * LANES, :] = b_slab(8 + i).T.astype(BF16)
    store_slabs(vb_ref, [_value_with_ones(b_slab(10 + i // 2), i % 2) for i in range(HEADS)])
    kr = rope(c_slab(19), 1)
    store_slabs(qc_ref, [rope(q[:, i * LANES:(i + 1) * LANES], 1) * ((C_NOPE + C_ROPE) ** -0.5 * LOG2E)
                         for i in range(HEADS)])
    for i in range(HEADS):
        kct_ref[i * LANES:(i + 1) * LANES, :] = (kv[:, i * LANES:(i + 1) * LANES] + kr).T.astype(BF16)
    store_slabs(vc_ref, [_value_with_ones(kv[:, (HEADS + i) * LANES:(HEADS + i + 1) * LANES], 0)
                         for i in range(HEADS)])
    qd = []
    for i in range(2):
        x = d_slab(12 + i)
        x = x * lax.rsqrt(_group_mean_sq(x, HEAD_DIM) + EPS) * dqn_ref[...]
        qd.append(rope(x, 2) * (HEAD_DIM ** -0.5 * LOG2E))
    store_slabs(qd_ref, qd)
    x = d_slab(14)
    x = x * lax.rsqrt(_group_mean_sq(x, HEAD_DIM) + EPS) * dkn_ref[...]
    kdt_ref[...] = rope(x, 2).T.astype(BF16)
    store_slabs(vd_ref, [_value_with_ones(d_slab(15), i) for i in range(D_KV_HEADS)])


def _inproj(x, g, w_mix, tables, cqn, ckvn, wuq, wukv, dqn, dkn):
    b = x.shape[0]
    tm = TM_IN
    grid = (SEQ // tm, b)
    tok = lambda c: pl.BlockSpec((None, tm, c), lambda j, i: (i, j, 0))
    tr = lambda c: pl.BlockSpec((None, c, tm), lambda j, i: (i, 0, j))
    full = lambda a: pl.BlockSpec(a.shape, lambda j, i: (0,) * a.ndim)
    out_cols = [("tok", 256), ("tr", 256), ("tok", 256), ("tok", 256), ("tr", 256), ("tok", 512),
                ("tok", 512), ("tr", 512), ("tok", 512), ("tok", 256), ("tr", 128), ("tok", 256)]
    out_specs = [tok(c) if k == "tok" else tr(c) for k, c in out_cols]
    out_shape = [jax.ShapeDtypeStruct((b, SEQ, c) if k == "tok" else (b, c, SEQ), BF16) for k, c in out_cols]
    return pl.pallas_call(
        _inproj_kernel,
        grid=grid,
        in_specs=[tok(D_MODEL), full(g), full(w_mix),
                  pl.BlockSpec((9, tm, LANES), lambda j, i: (0, j, 0)),
                  full(cqn), full(ckvn), full(wuq), full(wukv), full(dqn), full(dkn)],
        out_specs=out_specs,
        out_shape=out_shape,
        compiler_params=pltpu.CompilerParams(
            dimension_semantics=("arbitrary", "arbitrary"), vmem_limit_bytes=VMEM_LIMIT),
        name="inproj",
    )(x, g, w_mix, tables, cqn, ckvn, wuq, wukv, dqn, dkn)


def _exp2_terms(s):
    return jnp.exp2(s - jnp.max(s, axis=-1, keepdims=True))


def _normalise_by_ones_lanes(o):
    return o * (1.0 / pltpu.roll(o, HEAD_DIM, 1))


def _join_halves(y_low, y_high):
    return jnp.where(_low_half(y_low.shape[0]), y_low, pltpu.roll(y_high, HEAD_DIM, 1))


def _pipeline(jobs):
    n = len(jobs)
    s = {0: jobs[0][0]()}
    w = {}
    for j in range(n + 1):
        if j + 1 < n:
            s[j + 1] = jobs[j + 1][0]()
        if j >= 1:
            jobs[j - 1][2](w.pop(j - 1))
        if j < n:
            w[j] = jobs[j][1](s.pop(j))


def _softmax_weights(s):
    return _exp2_terms(s).astype(BF16)


def _diff_job(lam, q_ref, kt_ref, v_ref, g_ref, o_ref, rows, head, held, out_scale):
    pair, half = divmod(head, 2)
    n = rows.size

    def scores():
        return tuple(jnp.dot(q_ref[rows, A_D * j:A_D * (j + 1)], kt_ref[A_D * j:A_D * (j + 1), :],
                             preferred_element_type=F32) for j in (2 * head, 2 * head + 1))

    def weights(s):
        e0, e1 = _exp2_terms(s[0]), _exp2_terms(s[1])
        l0 = jnp.sum(e0, axis=-1, keepdims=True)
        l1 = jnp.sum(e1, axis=-1, keepdims=True)
        return (e0 - e1 * (lam * l0 / l1)).astype(BF16), l0

    def output(wl):
        w, l0 = wl
        o = jnp.dot(w, v_ref[:, pair * LANES:(pair + 1) * LANES], preferred_element_type=F32) * (1.0 / l0)
        mask = _low_half(n) if half == 0 else ~_low_half(n)
        ms = jnp.sum(jnp.where(mask, o * o, 0.0), axis=-1, keepdims=True) * (1.0 / HEAD_DIM)
        y = o * lax.rsqrt(ms + EPS)
        if half == 0:
            held[pair] = y
            return
        y = jnp.where(_low_half(n), held.pop(pair), y)
        o_ref[rows, pair * LANES:(pair + 1) * LANES] = (y * g_ref[...] * out_scale).astype(BF16)

    return scores, weights, output


def _softmax_job(q_ref, kt_ref, v_ref, o_ref, rows, q_cols, k_rows, v_slab, out_slab, half, held,
                 keys=slice(None), bias=None):
    def scores():
        s = jnp.dot(q_ref[rows, q_cols], kt_ref[k_rows, keys], preferred_element_type=F32)
        return s if bias is None else s + bias()

    def output(p):
        y = _normalise_by_ones_lanes(
            jnp.dot(p, v_ref[keys, v_slab * LANES:(v_slab + 1) * LANES], preferred_element_type=F32))
        if half == 0:
            held[out_slab] = y
            return
        o_ref[rows, out_slab * LANES:(out_slab + 1) * LANES] = _join_halves(held.pop(out_slab), y).astype(BF16)

    return scores, _softmax_weights, output


def _attn_acd_kernel(lam_ref, qa_ref, kat_ref, va_ref, ga_ref, qc_ref, kct_ref, vc_ref, qd_ref, kdt_ref, vd_ref,
                     oa_ref, oc_ref, od_ref, *, out_scale):
    lam = lam_ref[0]

    def tiles(i, carry):
        jobs = []
        for t in range(TILES_PER_ITER):
            rows = pl.ds(pl.multiple_of((i * TILES_PER_ITER + t) * TQ, TQ), TQ)
            held_a, held_c, held_d = {}, {}, {}
            for head in range(HEADS):
                lanes = slice(head * LANES, (head + 1) * LANES)
                kv_head, g = divmod(head, 2)
                jobs.append(_diff_job(lam, qa_ref, kat_ref, va_ref, ga_ref, oa_ref, rows, head, held_a, out_scale))
                jobs.append(_softmax_job(qc_ref, kct_ref, vc_ref, oc_ref, rows, lanes, lanes, head,
                                         head // 2, head % 2, held_c))
                jobs.append(_softmax_job(qd_ref, kdt_ref, vd_ref, od_ref, rows,
                                         slice(head * HEAD_DIM, (head + 1) * HEAD_DIM),
                                         slice(kv_head * HEAD_DIM, (kv_head + 1) * HEAD_DIM),
                                         kv_head, kv_head, g, held_d))
        _pipeline(jobs)
        return carry

    lax.fori_loop(0, SEQ // (TQ * TILES_PER_ITER), tiles, 0)


def _attn_acd(lam, qa, kat, va, ga, qc, kct, vc, qd, kdt, vd, out_scale):
    b = qa.shape[0]
    per_batch = lambda a: pl.BlockSpec((None,) + a.shape[1:], lambda i: (i, 0, 0))
    out = jax.ShapeDtypeStruct((b, SEQ, BRANCH_W), BF16)
    return pl.pallas_call(
        functools.partial(_attn_acd_kernel, out_scale=out_scale),
        grid=(b,),
        in_specs=[pl.BlockSpec(memory_space=pltpu.SMEM), per_batch(qa), per_batch(kat), per_batch(va),
                  pl.BlockSpec((1, LANES), lambda i: (0, 0)), per_batch(qc), per_batch(kct), per_batch(vc),
                  per_batch(qd), per_batch(kdt), per_batch(vd)],
        out_specs=[per_batch(out)] * 3,
        out_shape=[out] * 3,
        compiler_params=pltpu.CompilerParams(
            dimension_semantics=("arbitrary",), vmem_limit_bytes=VMEM_LIMIT),
        name="attn_acd",
    )(lam, qa, kat, va, ga, qc, kct, vc, qd, kdt, vd)


def _attn_b_kernel(q_ref, kt_ref, v_ref, bias_ref, o_ref):
    def steps(i, carry):
        jobs = []
        for t in range(NB_STEPS_PER_ITER):
            st = i * NB_STEPS_PER_ITER + t
            keys = pl.ds(pl.multiple_of(jnp.clip(st - 2, 0, NB_STEPS - 5) * NB_Q, NB_Q), NB_KEYS)
            variant = jnp.minimum(st, 2) + jnp.maximum(st - (NB_STEPS - 3), 0)
            rows = pl.ds(pl.multiple_of(st * NB_Q, NB_Q), NB_Q)
            held = {}
            for head in range(HEADS):
                cols = slice(head * HEAD_DIM, (head + 1) * HEAD_DIM)
                jobs.append(_softmax_job(q_ref, kt_ref, v_ref, o_ref, rows, cols, cols, head, head // 2, head % 2,
                                         held, keys=keys,
                                         bias=functools.partial(lambda v, h: bias_ref[v, h], variant, head)))
        _pipeline(jobs)
        return carry

    lax.fori_loop(0, NB_STEPS // NB_STEPS_PER_ITER, steps, 0)


def _attn_b(q, k, v, bias):
    b = q.shape[0]
    per_batch = lambda a: pl.BlockSpec((None,) + a.shape[1:], lambda i: (i, 0, 0))
    return pl.pallas_call(
        _attn_b_kernel,
        grid=(b,),
        in_specs=[per_batch(q), per_batch(k), per_batch(v),
                  pl.BlockSpec(bias.shape, lambda i: (0, 0, 0, 0), pipeline_mode=pl.Buffered(1))],
        out_specs=pl.BlockSpec((None, SEQ, BRANCH_W), lambda i: (i, 0, 0)),
        out_shape=jax.ShapeDtypeStruct((b, SEQ, BRANCH_W), BF16),
        compiler_params=pltpu.CompilerParams(
            dimension_semantics=("arbitrary",), vmem_limit_bytes=VMEM_LIMIT),
        name="attn_b",
    )(q, k, v, bias)


def _nb_bias_tables(rpb):
    rows = SEQ // GRID_W
    steps = np.array([0, 1, 2, NB_STEPS - 2, NB_STEPS - 1])
    start_row = np.clip(steps - 2, 0, NB_STEPS - 5) * 2
    qr = np.arange(2)
    kr = np.arange(NB_KEYS // GRID_W)
    r = 2 * steps[:, None, None] + qr[None, :, None]
    krow = start_row[:, None, None] + kr[None, None, :]
    rs = np.clip(r - WIN_R // 2, 0, rows - WIN_R)
    row_ok = (krow >= rs) & (krow < rs + WIN_R)
    dr = np.clip(krow - r + WIN_R - 1, 0, 2 * WIN_R - 2)
    qc = np.arange(GRID_W)[:, None]
    kc = np.arange(GRID_W)[None, :]
    cs = np.clip(qc - WIN_C // 2, 0, GRID_W - WIN_C)
    col_ok = (kc >= cs) & (kc < cs + WIN_C)
    dc = np.clip(kc - qc + WIN_C - 1, 0, 2 * WIN_C - 2)
    row_sel = np.eye(2 * WIN_R - 1, dtype=np.float32)[dr.reshape(-1)]
    col_sel = np.eye(2 * WIN_C - 1, dtype=np.float32)[dc.reshape(-1)].T
    hp = lax.Precision.HIGHEST
    t = jnp.einsum("ar,hrc->hac", row_sel, rpb.astype(F32), precision=hp)
    t = jnp.einsum("hac,ck->hak", t, col_sel, precision=hp)
    nkr = NB_KEYS // GRID_W
    t = t.reshape(HEADS, NB_VARIANTS, 2, nkr, GRID_W, GRID_W)
    t = jnp.transpose(t, (1, 0, 2, 4, 3, 5)).reshape(NB_VARIANTS, HEADS, NB_Q, NB_KEYS)
    valid = row_ok[:, :, None, :, None] & col_ok[None, None, :, None, :]
    valid = valid.reshape(NB_VARIANTS, 1, NB_Q, NB_KEYS)
    return jnp.where(jnp.asarray(valid), t * LOG2E, NEG)


def _merge_kernel(x_ref, g_ref, oa_ref, ob_ref, oc_ref, od_ref, wg_ref, wb_ref, wo_ref, y_ref):
    x = x_ref[...]
    h = _rms(x, g_ref[...]).astype(BF16)
    merged = None
    for i, o_ref in enumerate((oa_ref, ob_ref, oc_ref, od_ref)):
        gate = jax.nn.sigmoid(jnp.dot(h, wg_ref[:, i * D_MODEL:(i + 1) * D_MODEL], preferred_element_type=F32))
        term = gate * jnp.dot(o_ref[...], wb_ref[i], preferred_element_type=F32)
        merged = term if merged is None else merged + term
    y_ref[...] = x + jnp.dot(merged.astype(BF16), wo_ref[...], preferred_element_type=F32)


def _merge(x, g, oa, ob, oc, od, wg, wb, wo):
    t = x.shape[0]
    tm = TM_MERGE
    tok = lambda c: pl.BlockSpec((tm, c), lambda i: (i, 0))
    full = lambda a: pl.BlockSpec(a.shape, lambda i: (0,) * a.ndim)
    return pl.pallas_call(
        _merge_kernel,
        grid=(t // tm,),
        in_specs=[tok(D_MODEL), full(g), tok(BRANCH_W), tok(BRANCH_W), tok(BRANCH_W), tok(BRANCH_W),
                  full(wg), full(wb), full(wo)],
        out_specs=tok(D_MODEL),
        out_shape=jax.ShapeDtypeStruct((t, D_MODEL), F32),
        compiler_params=pltpu.CompilerParams(
            dimension_semantics=("arbitrary",), vmem_limit_bytes=VMEM_LIMIT),
        name="merge",
    )(x, g, oa, ob, oc, od, wg, wb, wo)


def _ffn_kernel(xp_ref, x_ref, xn_ref, g_ref, win_ref, cw_ref, cb_ref, wo_ref, fg_ref, y_ref, act_scr, *, final):
    j = pl.program_id(1)
    tm = x_ref.shape[0]
    ext = tm + 2 * SUBLANES
    x = x_ref[...]
    h = _rms(jnp.concatenate([xp_ref[...], x, xn_ref[...]], axis=0), g_ref[...]).astype(BF16)
    keep_prev = (j > 0).astype(F32)
    keep_next = (j < pl.num_programs(1) - 1).astype(F32)
    row = lax.broadcasted_iota(jnp.int32, (ext, 1), 0)
    halo_scale = jnp.where(row < SUBLANES, keep_prev, jnp.where(row >= tm + SUBLANES, keep_next, 1.0))
    for c in range(N_FFN_CHUNKS):
        cols = slice(c * FFN_CHUNK, (c + 1) * FFN_CHUNK)
        u = jnp.dot(h, win_ref[:, cols], preferred_element_type=F32)
        g = jnp.dot(h, win_ref[:, FFN_DIM + c * FFN_CHUNK:FFN_DIM + (c + 1) * FFN_CHUNK],
                    preferred_element_type=F32) * halo_scale
        gc = (pltpu.roll(g, 1, 0) * cw_ref[0:1, cols] + g * cw_ref[1:2, cols]
              + pltpu.roll(g, ext - 1, 0) * cw_ref[2:3, cols] + cb_ref[:, cols])
        act = jax.nn.silu(gc) * u
        act_scr[:, cols] = act[SUBLANES:SUBLANES + tm].astype(BF16)
    y = x + jnp.dot(act_scr[...], wo_ref[...], preferred_element_type=F32)
    if final:
        y = _rms(y, fg_ref[...])
    y_ref[...] = y


def _ffn(x, g, w_in, conv_w, conv_b, w_out, final_g, final):
    b = x.shape[0]
    tm = TM_FFN
    per = tm // SUBLANES
    last = SEQ // SUBLANES - 1
    const = lambda a: pl.BlockSpec(a.shape, lambda i, j: (0,) * a.ndim, pipeline_mode=pl.Buffered(1))
    return pl.pallas_call(
        functools.partial(_ffn_kernel, final=final),
        grid=(b, SEQ // tm),
        in_specs=[pl.BlockSpec((None, SUBLANES, D_MODEL), lambda i, j: (i, jnp.maximum(j * per - 1, 0), 0)),
                  pl.BlockSpec((None, tm, D_MODEL), lambda i, j: (i, j, 0)),
                  pl.BlockSpec((None, SUBLANES, D_MODEL), lambda i, j: (i, jnp.minimum((j + 1) * per, last), 0)),
                  const(g), const(w_in), const(conv_w), const(conv_b), const(w_out), const(final_g)],
        out_specs=pl.BlockSpec((None, tm, D_MODEL), lambda i, j: (i, j, 0)),
        out_shape=jax.ShapeDtypeStruct((b, SEQ, D_MODEL), F32),
        scratch_shapes=[pltpu.VMEM((tm, FFN_DIM), BF16)],
        compiler_params=pltpu.CompilerParams(
            dimension_semantics=("arbitrary", "arbitrary"), vmem_limit_bytes=VMEM_LIMIT),
        name="ffn",
    )(x, x, x, g, w_in, conv_w, conv_b, w_out, final_g)


def _rope_lane_tables():
    t = jnp.arange(SEQ)
    inv_freq = ROPE_THETA ** (-jnp.arange(0, 2 * ROPE_HALF, 2, dtype=F32) / (2 * ROPE_HALF))
    ang = lambda pos: pos.astype(F32)[:, None] * inv_freq[None, :]
    lane = np.arange(LANES)
    freq = lane % ROPE_HALF
    first = jnp.asarray((lane % (2 * ROPE_HALF)) < ROPE_HALF)[None, :]
    ang_seq = ang(t)[:, freq]
    ang_axial = jnp.where(jnp.asarray((lane // (2 * ROPE_HALF)) % 2 == 0)[None, :],
                          ang(t // GRID_W)[:, freq], ang(t % GRID_W)[:, freq])
    active = jnp.asarray((lane >= C_NOPE) & (lane < C_NOPE + C_ROPE))[None, :]

    def trio(a, on):
        cos, sin = jnp.cos(a), jnp.sin(a)
        return [jnp.where(on, cos, 1.0), jnp.where(on & first, -sin, 0.0), jnp.where(on & ~first, sin, 0.0)]

    everywhere = jnp.ones((1, LANES), bool)
    return jnp.stack(trio(ang_seq, everywhere) + trio(ang_seq, active) + trio(ang_axial, everywhere))


def _pack_layer(l, w_in, c_q_norm, c_kv_norm, c_w_uq, c_w_ukv, d_q_norm, d_k_norm):
    w = w_in[l]
    a, bb, cc, dd, gate = jnp.split(w, [768, 1536, 1888, 2400], axis=1)
    z = lambda n: jnp.zeros((D_MODEL, n), F32)
    cq, ckv, krope = cc[:, :C_QLORA], cc[:, C_QLORA:C_QLORA + C_KVLORA], cc[:, C_QLORA + C_KVLORA:]
    w_mix = jnp.concatenate([a, bb, dd, cq, z(64), ckv, z(C_NOPE), krope, z(LANES - C_NOPE - C_ROPE)], axis=1)
    assert w_mix.shape[1] == MIX_COLS
    wuq = c_w_uq[l].reshape(C_QLORA, HEADS, C_NOPE + C_ROPE)
    wuq = jnp.pad(wuq, ((0, 256 - C_QLORA), (0, 0), (0, LANES - C_NOPE - C_ROPE))).reshape(256, HEADS * LANES)
    wukv = c_w_ukv[l].reshape(C_KVLORA, HEADS, C_NOPE + HEAD_DIM)
    halves = ((0, 0), (0, 0), (0, LANES - HEAD_DIM))
    wk = jnp.pad(wukv[:, :, :C_NOPE], halves).reshape(C_KVLORA, HEADS * LANES)
    wv = jnp.pad(wukv[:, :, C_NOPE:], halves).reshape(C_KVLORA, HEADS * LANES)
    wukv = jnp.concatenate([wk, wv], axis=1)
    cqn = jnp.pad(c_q_norm[l], (0, 256 - C_QLORA))[None, :]
    return dict(
        w_mix=w_mix.astype(BF16), w_gate=gate.astype(BF16), wuq=wuq.astype(BF16), wukv=wukv.astype(BF16),
        cqn=cqn, ckvn=c_kv_norm[l][None, :],
        dqn=jnp.tile(d_q_norm[l], 2)[None, :], dkn=jnp.tile(d_k_norm[l], 2)[None, :])


def _trunk(x, tables, layers, final_norm):
    b = x.shape[0]
    for l, p in enumerate(layers):
        (qa, kat, va, qb, kb, vb, qc, kct, vc, qd, kdt, vd) = _inproj(
            x, p["attn_norm"], p["w_mix"], tables, p["cqn"], p["ckvn"], p["wuq"], p["wukv"], p["dqn"], p["dkn"])
        lam_init = 0.8 - 0.6 * math.exp(-0.3 * l)
        oa, oc, od = _attn_acd(p["lam"], qa, kat, va, p["subln"], qc, kct, vc, qd, kdt, vd, 1.0 - lam_init)
        ob = _attn_b(qb, kb, vb, p["nb_bias"])
        flat = lambda a: a.reshape(b * SEQ, a.shape[-1])
        x1 = _merge(flat(x), p["attn_norm"], flat(oa), flat(ob), flat(oc), flat(od),
                    p["w_gate"], p["w_branch"], p["w_out"]).reshape(b, SEQ, D_MODEL)
        x = _ffn(x1, p["ffn_norm"], p["w_ffn_in"], p["conv_w"], p["conv_b"], p["w_ffn_out"],
                 final_norm, final=(l == DEPTH - 1))
    return x


def kernel(x_prompt, x_sample, attn_norm, w_in, a_lambda_q1, a_lambda_k1, a_lambda_q2, a_lambda_k2, a_subln, b_rpb, c_q_norm, c_kv_norm, c_w_uq, c_w_ukv, d_q_norm, d_k_norm, w_branch, w_out, ffn_norm, w_ffn_in, ffn_conv_w, ffn_conv_b, w_ffn_out, final_norm):
    tables = _rope_lane_tables()
    layers = []
    for l in range(DEPTH):
        p = _pack_layer(l, w_in, c_q_norm, c_kv_norm, c_w_uq, c_w_ukv, d_q_norm, d_k_norm)
        lam_init = 0.8 - 0.6 * math.exp(-0.3 * l)
        lam = (jnp.exp(jnp.sum(a_lambda_q1[l] * a_lambda_k1[l]))
               - jnp.exp(jnp.sum(a_lambda_q2[l] * a_lambda_k2[l])) + lam_init)
        p.update(
            attn_norm=attn_norm[l][None, :], lam=lam.reshape(1).astype(F32),
            subln=jnp.tile(a_subln[l], 2)[None, :], nb_bias=_nb_bias_tables(b_rpb[l]),
            w_branch=w_branch[l].astype(BF16), w_out=w_out[l].astype(BF16),
            ffn_norm=ffn_norm[l][None, :], w_ffn_in=w_ffn_in[l].astype(BF16),
            conv_w=ffn_conv_w[l], conv_b=ffn_conv_b[l][None, :], w_ffn_out=w_ffn_out[l].astype(BF16))
        layers.append(p)
    fn = final_norm[None, :]
    return (_trunk(x_prompt, tables, layers, fn), _trunk(x_sample, tables, layers, fn))
```

```python
import functools
import math

import numpy as np
import jax
import jax.numpy as jnp
from jax import lax
from jax.experimental import pallas as pl
from jax.experimental.pallas import tpu as pltpu

F32 = jnp.float32
BF16 = jnp.bfloat16

D_MODEL = 1024
SEQ = 2048
DEPTH = 2
GRID_W = 64
ROPE_THETA = 10000.0
EPS = 1e-6
HEADS = 4
HEAD_DIM = 64
A_D = 32
WIN_R = 8
WIN_C = 16
C_NOPE = 64
C_ROPE = 32
C_QLORA = 192
C_KVLORA = 128
D_KV_HEADS = 2
BRANCH_W = 256
FFN_DIM = 2816
LANES = 128
SUBLANES = 8
ROPE_HALF = 16
LOG2E = 1.4426950408889634

MIX_COLS = 2560
TM_IN = 1024
TM_MERGE = 512
TQ = 256
TILES_DIFF = 2
TILES_PLAIN = 4
NB_STEPS_PER_ITER = 16
NB_Q = 2 * GRID_W
NB_KEYS = 10 * GRID_W
NB_STEPS = SEQ // NB_Q
NB_VARIANTS = 5
NEG = -1e30
TM_FFN = 512
FFN_CHUNK = 256
N_FFN_CHUNKS = FFN_DIM // FFN_CHUNK
VMEM_LIMIT = 56 * 1024 * 1024


def _rms(x, g):
    return x * lax.rsqrt(jnp.mean(x * x, axis=-1, keepdims=True) + EPS) * g


def _rope_slab(x, cos, sin_lo, sin_hi):
    return (x * cos + pltpu.roll(x, LANES - ROPE_HALF, 1) * sin_lo
            + pltpu.roll(x, ROPE_HALF, 1) * sin_hi)


def _group_mean_sq(x, width):
    sq = x * x
    hi = sq.astype(BF16)
    lo = (sq - hi.astype(F32)).astype(BF16)
    r = lax.broadcasted_iota(jnp.int32, (LANES, LANES), 0) // width
    c = lax.broadcasted_iota(jnp.int32, (LANES, LANES), 1) // width
    ones = jnp.where(r == c, 1.0, 0.0).astype(BF16)
    tot = (jnp.dot(hi, ones, preferred_element_type=F32)
           + jnp.dot(lo, ones, preferred_element_type=F32))
    return tot * (1.0 / width)


def _low_half(rows):
    return lax.broadcasted_iota(jnp.int32, (rows, LANES), 1) < HEAD_DIM


def _value_with_ones(v, shift):
    if shift:
        v = pltpu.roll(v, HEAD_DIM, 1)
    return jnp.where(_low_half(v.shape[0]), v, 1.0).astype(BF16)


def _inproj_kernel(x_ref, g_ref, w_ref, tab_ref, cqn_ref, ckvn_ref, wuq_ref, wukv_ref, dqn_ref, dkn_ref,
                   qa_ref, kat_ref, va_ref, qb_ref, kbt_ref, vb_ref,
                   qc_ref, kct_ref, vc_ref, qd_ref, kdt_ref, vd_ref):
    h = _rms(x_ref[...], g_ref[...]).astype(BF16)

    def project(first_slab, n_slabs):
        cols = slice(first_slab * LANES, (first_slab + n_slabs) * LANES)
        p = jnp.dot(h, w_ref[:, cols], preferred_element_type=F32)
        return lambda i: p[:, (i - first_slab) * LANES:(i - first_slab + 1) * LANES]

    def rope(x, variant):
        return _rope_slab(x, tab_ref[3 * variant], tab_ref[3 * variant + 1], tab_ref[3 * variant + 2])

    def store_slabs(ref, slabs):
        for i, v in enumerate(slabs):
            ref[:, i * LANES:(i + 1) * LANES] = v.astype(BF16)

    c_slab = project(16, 4)
    a_slab = project(0, 6)
    cq = jnp.concatenate([c_slab(16), c_slab(17)], axis=1)
    cq = cq * lax.rsqrt(jnp.sum(cq * cq, axis=-1, keepdims=True) * (1.0 / C_QLORA) + EPS) * cqn_ref[...]
    q = jnp.dot(cq.astype(BF16), wuq_ref[...], preferred_element_type=F32)
    ckv = _rms(c_slab(18), ckvn_ref[...])
    kv = jnp.dot(ckv.astype(BF16), wukv_ref[...], preferred_element_type=F32)
    b_slab = project(6, 6)
    store_slabs(qa_ref, [rope(a_slab(i), 0) * (A_D ** -0.5 * LOG2E) for i in range(2)])
    for i in range(2):
        kat_ref[i * LANES:(i + 1) * LANES, :] = rope(a_slab(2 + i), 0).T.astype(BF16)
    store_slabs(va_ref, [a_slab(4), a_slab(5)])
    d_slab = project(12, 4)
    store_slabs(qb_ref, [b_slab(6 + i) * (HEAD_DIM ** -0.5 * LOG2E) for i in range(2)])
    for i in range(2):
        kbt_ref[i * LANES:(i + 1) * LANES, :] = b_slab(8 + i).T.astype(BF16)
    store_slabs(vb_ref, [_value_with_ones(b_slab(10 + i // 2), i % 2) for i in range(HEADS)])
    kr = rope(c_slab(19), 1)
    store_slabs(qc_ref, [rope(q[:, i * LANES:(i + 1) * LANES], 1) * ((C_NOPE + C_ROPE) ** -0.5 * LOG2E)
                         for i in range(HEADS)])
    for i in range(HEADS):
        kct_ref[i * LANES:(i + 1) * LANES, :] = (kv[:, i * LANES:(i + 1) * LANES] + kr).T.astype(BF16)
    store_slabs(vc_ref, [_value_with_ones(kv[:, (HEADS + i) * LANES:(HEADS + i + 1) * LANES], 0)
                         for i in range(HEADS)])
    qd = []
    for i in range(2):
        x = d_slab(12 + i)
        x = x * lax.rsqrt(_group_mean_sq(x, HEAD_DIM) + EPS) * dqn_ref[...]
        qd.append(rope(x, 2) * (HEAD_DIM ** -0.5 * LOG2E))
    store_slabs(qd_ref, qd)
    x = d_slab(14)
    x = x * lax.rsqrt(_group_mean_sq(x, HEAD_DIM) + EPS) * dkn_ref[...]
    kdt_ref[...] = rope(x, 2).T.astype(BF16)
    store_slabs(vd_ref, [_value_with_ones(d_slab(15), i) for i in range(D_KV_HEADS)])


def _inproj(x, g, w_mix, tables, cqn, ckvn, wuq, wukv, dqn, dkn):
    b = x.shape[0]
    tm = TM_IN
    grid = (SEQ // tm, b)
    tok = lambda c: pl.BlockSpec((None, tm, c), lambda j, i: (i, j, 0))
    tr = lambda c: pl.BlockSpec((None, c, tm), lambda j, i: (i, 0, j))
    full = lambda a: pl.BlockSpec(a.shape, lambda j, i: (0,) * a.ndim)
    out_cols = [("tok", 256), ("tr", 256), ("tok", 256), ("tok", 256), ("tr", 256), ("tok", 512),
                ("tok", 512), ("tr", 512), ("tok", 512), ("tok", 256), ("tr", 128), ("tok", 256)]
    out_specs = [tok(c) if k == "tok" else tr(c) for k, c in out_cols]
    out_shape = [jax.ShapeDtypeStruct((b, SEQ, c) if k == "tok" else (b, c, SEQ), BF16) for k, c in out_cols]
    return pl.pallas_call(
        _inproj_kernel,
        grid=grid,
        in_specs=[tok(D_MODEL), full(g), full(w_mix),
                  pl.BlockSpec((9, tm, LANES), lambda j, i: (0, j, 0)),
                  full(cqn), full(ckvn), full(wuq), full(wukv), full(dqn), full(dkn)],
        out_specs=out_specs,
        out_shape=out_shape,
        compiler_params=pltpu.CompilerParams(
            dimension_semantics=("arbitrary", "arbitrary"), vmem_limit_bytes=VMEM_LIMIT),
        name="inproj",
    )(x, g, w_mix, tables, cqn, ckvn, wuq, wukv, dqn, dkn)


def _exp2_terms(s):
    return jnp.exp2(s - jnp.max(s, axis=-1, keepdims=True))


def _softmax_weights(job, s):
    return _exp2_terms(s).astype(BF16)


def _normalise_by_ones_lanes(o):
    return o * (1.0 / pltpu.roll(o, HEAD_DIM, 1))


def _join_halves(y_low, y_high):
    return jnp.where(_low_half(y_low.shape[0]), y_low, pltpu.roll(y_high, HEAD_DIM, 1))


def _pipeline(n, scores, weights, output):
    s = {0: scores(0)}
    w = {}
    for j in range(n + 1):
        if j + 1 < n:
            s[j + 1] = scores(j + 1)
        if j >= 1:
            output(j - 1, w.pop(j - 1))
        if j < n:
            w[j] = weights(j, s.pop(j))


def _tile_rows(i, t, tiles):
    return pl.ds(pl.multiple_of((i * tiles + t) * TQ, TQ), TQ)


def _attn_a_kernel(lam_ref, q_ref, kt_ref, v_ref, g_ref, o_ref, *, out_scale):
    lam = lam_ref[0]

    def tiles(i, carry):
        def scores(job):
            t, head = divmod(job, HEADS)
            return tuple(
                jnp.dot(q_ref[_tile_rows(i, t, TILES_DIFF), A_D * j:A_D * (j + 1)], kt_ref[A_D * j:A_D * (j + 1), :],
                        preferred_element_type=F32) for j in (2 * head, 2 * head + 1))

        def weights(job, s):
            e0, e1 = _exp2_terms(s[0]), _exp2_terms(s[1])
            l0 = jnp.sum(e0, axis=-1, keepdims=True)
            l1 = jnp.sum(e1, axis=-1, keepdims=True)
            return (e0 - e1 * (lam * l0 / l1)).astype(BF16), l0

        held = {}

        def output(job, wl):
            t, head = divmod(job, HEADS)
            pair, half = divmod(head, 2)
            w, l0 = wl
            o = jnp.dot(w, v_ref[:, pair * LANES:(pair + 1) * LANES], preferred_element_type=F32) * (1.0 / l0)
            mask = _low_half(TQ) if half == 0 else ~_low_half(TQ)
            ms = jnp.sum(jnp.where(mask, o * o, 0.0), axis=-1, keepdims=True) * (1.0 / HEAD_DIM)
            y = o * lax.rsqrt(ms + EPS)
            if half == 0:
                held["low"] = y
                return
            y = jnp.where(_low_half(TQ), held.pop("low"), y)
            o_ref[_tile_rows(i, t, TILES_DIFF), pair * LANES:(pair + 1) * LANES] = (y * g_ref[...] * out_scale).astype(BF16)

        _pipeline(TILES_DIFF * HEADS, scores, weights, output)
        return carry

    lax.fori_loop(0, SEQ // (TQ * TILES_DIFF), tiles, 0)


def _attn_c_kernel(q_ref, kt_ref, v_ref, o_ref):
    def tiles(i, carry):
        def scores(job):
            t, head = divmod(job, HEADS)
            cols = slice(head * LANES, (head + 1) * LANES)
            return jnp.dot(q_ref[_tile_rows(i, t, TILES_PLAIN), cols], kt_ref[cols, :], preferred_element_type=F32)

        held = {}

        def output(job, p):
            t, head = divmod(job, HEADS)
            pair, half = divmod(head, 2)
            y = _normalise_by_ones_lanes(
                jnp.dot(p, v_ref[:, head * LANES:(head + 1) * LANES], preferred_element_type=F32))
            if half == 0:
                held["low"] = y
                return
            o_ref[_tile_rows(i, t, TILES_PLAIN), pair * LANES:(pair + 1) * LANES] = _join_halves(held.pop("low"), y).astype(BF16)

        _pipeline(TILES_PLAIN * HEADS, scores, _softmax_weights, output)
        return carry

    lax.fori_loop(0, SEQ // (TQ * TILES_PLAIN), tiles, 0)


def _attn_d_kernel(q_ref, kt_ref, v_ref, o_ref):
    def tiles(i, carry):
        def scores(job):
            t, head = divmod(job, HEADS)
            kv_head = head // 2
            return jnp.dot(q_ref[_tile_rows(i, t, TILES_PLAIN), head * HEAD_DIM:(head + 1) * HEAD_DIM],
                           kt_ref[kv_head * HEAD_DIM:(kv_head + 1) * HEAD_DIM, :], preferred_element_type=F32)

        held = {}

        def output(job, p):
            t, head = divmod(job, HEADS)
            kv_head, g = divmod(head, 2)
            y = _normalise_by_ones_lanes(
                jnp.dot(p, v_ref[:, kv_head * LANES:(kv_head + 1) * LANES], preferred_element_type=F32))
            if g == 0:
                held["low"] = y
                return
            o_ref[_tile_rows(i, t, TILES_PLAIN), kv_head * LANES:(kv_head + 1) * LANES] = (
                _join_halves(held.pop("low"), y).astype(BF16))

        _pipeline(TILES_PLAIN * HEADS, scores, _softmax_weights, output)
        return carry

    lax.fori_loop(0, SEQ // (TQ * TILES_PLAIN), tiles, 0)


def _attn_full(kernel, q, kt, v, extra_in=(), extra_specs=(), scalar=None, name=None):
    b = q.shape[0]
    per_batch = lambda a: pl.BlockSpec((None,) + a.shape[1:], lambda i: (i, 0, 0))
    in_specs = [per_batch(q), per_batch(kt), per_batch(v)]
    args = [q, kt, v]
    if scalar is not None:
        in_specs = [pl.BlockSpec(memory_space=pltpu.SMEM)] + in_specs
        args = [scalar] + args
    in_specs += list(extra_specs)
    args += list(extra_in)
    return pl.pallas_call(
        kernel,
        grid=(b,),
        in_specs=in_specs,
        out_specs=pl.BlockSpec((None, SEQ, BRANCH_W), lambda i: (i, 0, 0)),
        out_shape=jax.ShapeDtypeStruct((b, SEQ, BRANCH_W), BF16),
        compiler_params=pltpu.CompilerParams(
            dimension_semantics=("arbitrary",), vmem_limit_bytes=VMEM_LIMIT),
        name=name,
    )(*args)


def _attn_b_kernel(q_ref, kt_ref, v_ref, bias_ref, o_ref):
    def steps(i, carry):
        def window(t):
            st = i * NB_STEPS_PER_ITER + t
            start = pl.multiple_of(jnp.clip(st - 2, 0, NB_STEPS - 5) * NB_Q, NB_Q)
            variant = jnp.minimum(st, 2) + jnp.maximum(st - (NB_STEPS - 3), 0)
            return pl.ds(pl.multiple_of(st * NB_Q, NB_Q), NB_Q), pl.ds(start, NB_KEYS), variant

        def scores(job):
            t, head = divmod(job, HEADS)
            rows, keys, variant = window(t)
            cols = slice(head * HEAD_DIM, (head + 1) * HEAD_DIM)
            s = jnp.dot(q_ref[rows, cols], kt_ref[cols, keys], preferred_element_type=F32)
            return s + bias_ref[variant, head]

        held = {}

        def output(job, p):
            t, head = divmod(job, HEADS)
            rows, keys, _ = window(t)
            pair, half = divmod(head, 2)
            y = _normalise_by_ones_lanes(
                jnp.dot(p, v_ref[keys, head * LANES:(head + 1) * LANES], preferred_element_type=F32))
            if half == 0:
                held["low"] = y
                return
            o_ref[rows, pair * LANES:(pair + 1) * LANES] = _join_halves(held.pop("low"), y).astype(BF16)

        _pipeline(NB_STEPS_PER_ITER * HEADS, scores, _softmax_weights, output)
        return carry

    lax.fori_loop(0, NB_STEPS // NB_STEPS_PER_ITER, steps, 0)


def _attn_b(q, k, v, bias):
    b = q.shape[0]
    per_batch = lambda a: pl.BlockSpec((None,) + a.shape[1:], lambda i: (i, 0, 0))
    return pl.pallas_call(
        _attn_b_kernel,
        grid=(b,),
        in_specs=[per_batch(q), per_batch(k), per_batch(v),
                  pl.BlockSpec(bias.shape, lambda i: (0, 0, 0, 0), pipeline_mode=pl.Buffered(1))],
        out_specs=pl.BlockSpec((None, SEQ, BRANCH_W), lambda i: (i, 0, 0)),
        out_shape=jax.ShapeDtypeStruct((b, SEQ, BRANCH_W), BF16),
        compiler_params=pltpu.CompilerParams(
            dimension_semantics=("arbitrary",), vmem_limit_bytes=VMEM_LIMIT),
        name="attn_b",
    )(q, k, v, bias)


def _nb_bias_tables(rpb):
    rows = SEQ // GRID_W
    steps = np.array([0, 1, 2, NB_STEPS - 2, NB_STEPS - 1])
    start_row = np.clip(steps - 2, 0, NB_STEPS - 5) * 2
    qr = np.arange(2)
    kr = np.arange(NB_KEYS // GRID_W)
    r = 2 * steps[:, None, None] + qr[None, :, None]
    krow = start_row[:, None, None] + kr[None, None, :]
    rs = np.clip(r - WIN_R // 2, 0, rows - WIN_R)
    row_ok = (krow >= rs) & (krow < rs + WIN_R)
    dr = np.clip(krow - r + WIN_R - 1, 0, 2 * WIN_R - 2)
    qc = np.arange(GRID_W)[:, None]
    kc = np.arange(GRID_W)[None, :]
    cs = np.clip(qc - WIN_C // 2, 0, GRID_W - WIN_C)
    col_ok = (kc >= cs) & (kc < cs + WIN_C)
    dc = np.clip(kc - qc + WIN_C - 1, 0, 2 * WIN_C - 2)
    row_sel = np.eye(2 * WIN_R - 1, dtype=np.float32)[dr.reshape(-1)]
    col_sel = np.eye(2 * WIN_C - 1, dtype=np.float32)[dc.reshape(-1)].T
    hp = lax.Precision.HIGHEST
    t = jnp.einsum("ar,hrc->hac", row_sel, rpb.astype(F32), precision=hp)
    t = jnp.einsum("hac,ck->hak", t, col_sel, precision=hp)
    nkr = NB_KEYS // GRID_W
    t = t.reshape(HEADS, NB_VARIANTS, 2, nkr, GRID_W, GRID_W)
    t = jnp.transpose(t, (1, 0, 2, 4, 3, 5)).reshape(NB_VARIANTS, HEADS, NB_Q, NB_KEYS)
    valid = row_ok[:, :, None, :, None] & col_ok[None, None, :, None, :]
    valid = valid.reshape(NB_VARIANTS, 1, NB_Q, NB_KEYS)
    return jnp.where(jnp.asarray(valid), t * LOG2E, NEG)


def _merge_kernel(x_ref, g_ref, oa_ref, ob_ref, oc_ref, od_ref, wg_ref, wb_ref, wo_ref, y_ref):
    x = x_ref[...]
    h = _rms(x, g_ref[...]).astype(BF16)
    merged = None
    for i, o_ref in enumerate((oa_ref, ob_ref, oc_ref, od_ref)):
        gate = jax.nn.sigmoid(jnp.dot(h, wg_ref[:, i * D_MODEL:(i + 1) * D_MODEL], preferred_element_type=F32))
        term = gate * jnp.dot(o_ref[...], wb_ref[i], preferred_element_type=F32)
        merged = term if merged is None else merged + term
    y_ref[...] = x + jnp.dot(merged.astype(BF16), wo_ref[...], preferred_element_type=F32)


def _merge(x, g, oa, ob, oc, od, wg, wb, wo):
    t = x.shape[0]
    tm = TM_MERGE
    tok = lambda c: pl.BlockSpec((tm, c), lambda i: (i, 0))
    full = lambda a: pl.BlockSpec(a.shape, lambda i: (0,) * a.ndim)
    return pl.pallas_call(
        _merge_kernel,
        grid=(t // tm,),
        in_specs=[tok(D_MODEL), full(g), tok(BRANCH_W), tok(BRANCH_W), tok(BRANCH_W), tok(BRANCH_W),
                  full(wg), full(wb), full(wo)],
        out_specs=tok(D_MODEL),
        out_shape=jax.ShapeDtypeStruct((t, D_MODEL), F32),
        compiler_params=pltpu.CompilerParams(
            dimension_semantics=("arbitrary",), vmem_limit_bytes=VMEM_LIMIT),
        name="merge",
    )(x, g, oa, ob, oc, od, wg, wb, wo)


def _ffn_kernel(xp_ref, x_ref, xn_ref, g_ref, win_ref, cw_ref, cb_ref, wo_ref, fg_ref, y_ref, act_scr, *, final):
    j = pl.program_id(1)
    tm = x_ref.shape[0]
    ext = tm + 2 * SUBLANES
    x = x_ref[...]
    h = _rms(jnp.concatenate([xp_ref[...], x, xn_ref[...]], axis=0), g_ref[...]).astype(BF16)
    keep_prev = (j > 0).astype(F32)
    keep_next = (j < pl.num_programs(1) - 1).astype(F32)
    row = lax.broadcasted_iota(jnp.int32, (ext, 1), 0)
    halo_scale = jnp.where(row < SUBLANES, keep_prev, jnp.where(row >= tm + SUBLANES, keep_next, 1.0))
    for c in range(N_FFN_CHUNKS):
        cols = slice(c * FFN_CHUNK, (c + 1) * FFN_CHUNK)
        u = jnp.dot(h, win_ref[:, cols], preferred_element_type=F32)
        g = jnp.dot(h, win_ref[:, FFN_DIM + c * FFN_CHUNK:FFN_DIM + (c + 1) * FFN_CHUNK],
                    preferred_element_type=F32) * halo_scale
        gc = (pltpu.roll(g, 1, 0) * cw_ref[0:1, cols] + g * cw_ref[1:2, cols]
              + pltpu.roll(g, ext - 1, 0) * cw_ref[2:3, cols] + cb_ref[:, cols])
        act = jax.nn.silu(gc) * u
        act_scr[:, cols] = act[SUBLANES:SUBLANES + tm].astype(BF16)
    y = x + jnp.dot(act_scr[...], wo_ref[...], preferred_element_type=F32)
    if final:
        y = _rms(y, fg_ref[...])
    y_ref[...] = y


def _ffn(x, g, w_in, conv_w, conv_b, w_out, final_g, final):
    b = x.shape[0]
    tm = TM_FFN
    per = tm // SUBLANES
    last = SEQ // SUBLANES - 1
    const = lambda a: pl.BlockSpec(a.shape, lambda i, j: (0,) * a.ndim, pipeline_mode=pl.Buffered(1))
    return pl.pallas_call(
        functools.partial(_ffn_kernel, final=final),
        grid=(b, SEQ // tm),
        in_specs=[pl.BlockSpec((None, SUBLANES, D_MODEL), lambda i, j: (i, jnp.maximum(j * per - 1, 0), 0)),
                  pl.BlockSpec((None, tm, D_MODEL), lambda i, j: (i, j, 0)),
                  pl.BlockSpec((None, SUBLANES, D_MODEL), lambda i, j: (i, jnp.minimum((j + 1) * per, last), 0)),
                  const(g), const(w_in), const(conv_w), const(conv_b), const(w_out), const(final_g)],
        out_specs=pl.BlockSpec((None, tm, D_MODEL), lambda i, j: (i, j, 0)),
        out_shape=jax.ShapeDtypeStruct((b, SEQ, D_MODEL), F32),
        scratch_shapes=[pltpu.VMEM((tm, FFN_DIM), BF16)],
        compiler_params=pltpu.CompilerParams(
            dimension_semantics=("arbitrary", "arbitrary"), vmem_limit_bytes=VMEM_LIMIT),
        name="ffn",
    )(x, x, x, g, w_in, conv_w, conv_b, w_out, final_g)


def _rope_lane_tables():
    t = jnp.arange(SEQ)
    inv_freq = ROPE_THETA ** (-jnp.arange(0, 2 * ROPE_HALF, 2, dtype=F32) / (2 * ROPE_HALF))
    ang = lambda pos: pos.astype(F32)[:, None] * inv_freq[None, :]
    lane = np.arange(LANES)
    freq = lane % ROPE_HALF
    first = jnp.asarray((lane % (2 * ROPE_HALF)) < ROPE_HALF)[None, :]
    ang_seq = ang(t)[:, freq]
    ang_axial = jnp.where(jnp.asarray((lane // (2 * ROPE_HALF)) % 2 == 0)[None, :],
                          ang(t // GRID_W)[:, freq], ang(t % GRID_W)[:, freq])
    active = jnp.asarray((lane >= C_NOPE) & (lane < C_NOPE + C_ROPE))[None, :]

    def trio(a, on):
        cos, sin = jnp.cos(a), jnp.sin(a)
        return [jnp.where(on, cos, 1.0), jnp.where(on & first, -sin, 0.0), jnp.where(on & ~first, sin, 0.0)]

    everywhere = jnp.ones((1, LANES), bool)
    return jnp.stack(trio(ang_seq, everywhere) + trio(ang_seq, active) + trio(ang_axial, everywhere))


def _pack_layer(l, w_in, c_q_norm, c_kv_norm, c_w_uq, c_w_ukv, d_q_norm, d_k_norm):
    w = w_in[l]
    a, bb, cc, dd, gate = jnp.split(w, [768, 1536, 1888, 2400], axis=1)
    z = lambda n: jnp.zeros((D_MODEL, n), F32)
    cq, ckv, krope = cc[:, :C_QLORA], cc[:, C_QLORA:C_QLORA + C_KVLORA], cc[:, C_QLORA + C_KVLORA:]
    w_mix = jnp.concatenate([a, bb, dd, cq, z(64), ckv, z(C_NOPE), krope, z(LANES - C_NOPE - C_ROPE)], axis=1)
    assert w_mix.shape[1] == MIX_COLS
    wuq = c_w_uq[l].reshape(C_QLORA, HEADS, C_NOPE + C_ROPE)
    wuq = jnp.pad(wuq, ((0, 256 - C_QLORA), (0, 0), (0, LANES - C_NOPE - C_ROPE))).reshape(256, HEADS * LANES)
    wukv = c_w_ukv[l].reshape(C_KVLORA, HEADS, C_NOPE + HEAD_DIM)
    halves = ((0, 0), (0, 0), (0, LANES - HEAD_DIM))
    wk = jnp.pad(wukv[:, :, :C_NOPE], halves).reshape(C_KVLORA, HEADS * LANES)
    wv = jnp.pad(wukv[:, :, C_NOPE:], halves).reshape(C_KVLORA, HEADS * LANES)
    wukv = jnp.concatenate([wk, wv], axis=1)
    cqn = jnp.pad(c_q_norm[l], (0, 256 - C_QLORA))[None, :]
    return dict(
        w_mix=w_mix.astype(BF16), w_gate=gate.astype(BF16), wuq=wuq.astype(BF16), wukv=wukv.astype(BF16),
        cqn=cqn, ckvn=c_kv_norm[l][None, :],
        dqn=jnp.tile(d_q_norm[l], 2)[None, :], dkn=jnp.tile(d_k_norm[l], 2)[None, :])


def _trunk(x, tables, layers, final_norm):
    b = x.shape[0]
    for l, p in enumerate(layers):
        (qa, kat, va, qb, kb, vb, qc, kct, vc, qd, kdt, vd) = _inproj(
            x, p["attn_norm"], p["w_mix"], tables, p["cqn"], p["ckvn"], p["wuq"], p["wukv"], p["dqn"], p["dkn"])
        lam_init = 0.8 - 0.6 * math.exp(-0.3 * l)
        oa = _attn_full(functools.partial(_attn_a_kernel, out_scale=1.0 - lam_init), qa, kat, va,
                        extra_in=(p["subln"],), extra_specs=(pl.BlockSpec((1, LANES), lambda i: (0, 0)),),
                        scalar=p["lam"], name="attn_a")
        ob = _attn_b(qb, kb, vb, p["nb_bias"])
        oc = _attn_full(_attn_c_kernel, qc, kct, vc, name="attn_c")
        od = _attn_full(_attn_d_kernel, qd, kdt, vd, name="attn_d")
        flat = lambda a: a.reshape(b * SEQ, a.shape[-1])
        x1 = _merge(flat(x), p["attn_norm"], flat(oa), flat(ob), flat(oc), flat(od),
                    p["w_gate"], p["w_branch"], p["w_out"]).reshape(b, SEQ, D_MODEL)
        x = _ffn(x1, p["ffn_norm"], p["w_ffn_in"], p["conv_w"], p["conv_b"], p["w_ffn_out"],
                 final_norm, final=(l == DEPTH - 1))
    return x


def kernel(x_prompt, x_sample, attn_norm, w_in, a_lambda_q1, a_lambda_k1, a_lambda_q2, a_lambda_k2, a_subln, b_rpb, c_q_norm, c_kv_norm, c_w_uq, c_w_ukv, d_q_norm, d_k_norm, w_branch, w_out, ffn_norm, w_ffn_in, ffn_conv_w, ffn_conv_b, w_ffn_out, final_norm):
    tables = _rope_lane_tables()
    layers = []
    for l in range(DEPTH):
        p = _pack_layer(l, w_in, c_q_norm, c_kv_norm, c_w_uq, c_w_ukv, d_q_norm, d_k_norm)
        lam_init = 0.8 - 0.6 * math.exp(-0.3 * l)
        lam = (jnp.exp(jnp.sum(a_lambda_q1[l] * a_lambda_k1[l]))
               - jnp.exp(jnp.sum(a_lambda_q2[l] * a_lambda_k2[l])) + lam_init)
        p.update(
            attn_norm=attn_norm[l][None, :], lam=lam.reshape(1).astype(F32),
            subln=jnp.tile(a_subln[l], 2)[None, :], nb_bias=_nb_bias_tables(b_rpb[l]),
            w_branch=w_branch[l].astype(BF16), w_out=w_out[l].astype(BF16),
            ffn_norm=ffn_norm[l][None, :], w_ffn_in=w_ffn_in[l].astype(BF16),
            conv_w=ffn_conv_w[l], conv_b=ffn_conv_b[l][None, :], w_ffn_out=w_ffn_out[l].astype(BF16))
        layers.append(p)
    fn = final_norm[None, :]
    return (_trunk(x_prompt, tables, layers, fn), _trunk(x_sample, tables, layers, fn))
```

```python
import functools
import math

import numpy as np
import jax
import jax.numpy as jnp
from jax import lax
from jax.experimental import pallas as pl
from jax.experimental.pallas import tpu as pltpu

F32 = jnp.float32
BF16 = jnp.bfloat16

D_MODEL = 1024
SEQ = 2048
DEPTH = 2
GRID_W = 64
ROPE_THETA = 10000.0
EPS = 1e-6
HEADS = 4
HEAD_DIM = 64
A_D = 32
WIN_R = 8
WIN_C = 16
C_NOPE = 64
C_ROPE = 32
C_QLORA = 192
C_KVLORA = 128
D_KV_HEADS = 2
BRANCH_W = 256
FFN_DIM = 2816
LANES = 128
SUBLANES = 8
ROPE_HALF = 16
LOG2E = 1.4426950408889634

MIX_COLS = 2560
TM_IN = 1024
TM_MERGE = 512
TQ_DIFF = 512
TILES_DIFF = 1
TQ_PLAIN = 512
TILES_PLAIN = 2
NB_STEPS_PER_ITER = 16
NB_Q = 2 * GRID_W
NB_KEYS = 10 * GRID_W
NB_STEPS = SEQ // NB_Q
NB_VARIANTS = 5
NEG = -1e30
TM_FFN = 512
FFN_CHUNK = 256
N_FFN_CHUNKS = FFN_DIM // FFN_CHUNK
VMEM_LIMIT = 56 * 1024 * 1024


def _rms(x, g):
    return x * lax.rsqrt(jnp.mean(x * x, axis=-1, keepdims=True) + EPS) * g


def _rope_slab(x, cos, sin_lo, sin_hi):
    return (x * cos + pltpu.roll(x, LANES - ROPE_HALF, 1) * sin_lo
            + pltpu.roll(x, ROPE_HALF, 1) * sin_hi)


def _group_mean_sq(x, width):
    sq = x * x
    hi = sq.astype(BF16)
    lo = (sq - hi.astype(F32)).astype(BF16)
    r = lax.broadcasted_iota(jnp.int32, (LANES, LANES), 0) // width
    c = lax.broadcasted_iota(jnp.int32, (LANES, LANES), 1) // width
    ones = jnp.where(r == c, 1.0, 0.0).astype(BF16)
    tot = (jnp.dot(hi, ones, preferred_element_type=F32)
           + jnp.dot(lo, ones, preferred_element_type=F32))
    return tot * (1.0 / width)


def _low_half(rows):
    return lax.broadcasted_iota(jnp.int32, (rows, LANES), 1) < HEAD_DIM


def _value_with_ones(v, shift):
    if shift:
        v = pltpu.roll(v, HEAD_DIM, 1)
    return jnp.where(_low_half(v.shape[0]), v, 1.0).astype(BF16)


def _inproj_kernel(x_ref, g_ref, w_ref, tab_ref, cqn_ref, ckvn_ref, wuq_ref, wukv_ref, dqn_ref, dkn_ref,
                   qa_ref, kat_ref, va_ref, qb_ref, kbt_ref, vb_ref,
                   qc_ref, kct_ref, vc_ref, qd_ref, kdt_ref, vd_ref):
    h = _rms(x_ref[...], g_ref[...]).astype(BF16)

    def project(first_slab, n_slabs):
        cols = slice(first_slab * LANES, (first_slab + n_slabs) * LANES)
        p = jnp.dot(h, w_ref[:, cols], preferred_element_type=F32)
        return lambda i: p[:, (i - first_slab) * LANES:(i - first_slab + 1) * LANES]

    def rope(x, variant):
        return _rope_slab(x, tab_ref[3 * variant], tab_ref[3 * variant + 1], tab_ref[3 * variant + 2])

    def store_slabs(ref, slabs):
        for i, v in enumerate(slabs):
            ref[:, i * LANES:(i + 1) * LANES] = v.astype(BF16)

    c_slab = project(16, 4)
    a_slab = project(0, 6)
    cq = jnp.concatenate([c_slab(16), c_slab(17)], axis=1)
    cq = cq * lax.rsqrt(jnp.sum(cq * cq, axis=-1, keepdims=True) * (1.0 / C_QLORA) + EPS) * cqn_ref[...]
    q = jnp.dot(cq.astype(BF16), wuq_ref[...], preferred_element_type=F32)
    ckv = _rms(c_slab(18), ckvn_ref[...])
    kv = jnp.dot(ckv.astype(BF16), wukv_ref[...], preferred_element_type=F32)
    b_slab = project(6, 6)
    store_slabs(qa_ref, [rope(a_slab(i), 0) * (A_D ** -0.5 * LOG2E) for i in range(2)])
    for i in range(2):
        kat_ref[i * LANES:(i + 1) * LANES, :] = rope(a_slab(2 + i), 0).T.astype(BF16)
    store_slabs(va_ref, [a_slab(4), a_slab(5)])
    d_slab = project(12, 4)
    store_slabs(qb_ref, [b_slab(6 + i) * (HEAD_DIM ** -0.5 * LOG2E) for i in range(2)])
    for i in range(2):
        kbt_ref[i * LANES:(i + 1) * LANES, :] = b_slab(8 + i).T.astype(BF16)
    store_slabs(vb_ref, [_value_with_ones(b_slab(10 + i // 2), i % 2) for i in range(HEADS)])
    kr = rope(c_slab(19), 1)
    store_slabs(qc_ref, [rope(q[:, i * LANES:(i + 1) * LANES], 1) * ((C_NOPE + C_ROPE) ** -0.5 * LOG2E)
                         for i in range(HEADS)])
    for i in range(HEADS):
        kct_ref[i * LANES:(i + 1) * LANES, :] = (kv[:, i * LANES:(i + 1) * LANES] + kr).T.astype(BF16)
    store_slabs(vc_ref, [_value_with_ones(kv[:, (HEADS + i) * LANES:(HEADS + i + 1) * LANES], 0)
                         for i in range(HEADS)])
    qd = []
    for i in range(2):
        x = d_slab(12 + i)
        x = x * lax.rsqrt(_group_mean_sq(x, HEAD_DIM) + EPS) * dqn_ref[...]
        qd.append(rope(x, 2) * (HEAD_DIM ** -0.5 * LOG2E))
    store_slabs(qd_ref, qd)
    x = d_slab(14)
    x = x * lax.rsqrt(_group_mean_sq(x, HEAD_DIM) + EPS) * dkn_ref[...]
    kdt_ref[...] = rope(x, 2).T.astype(BF16)
    store_slabs(vd_ref, [_value_with_ones(d_slab(15), i) for i in range(D_KV_HEADS)])


def _inproj(x, g, w_mix, tables, cqn, ckvn, wuq, wukv, dqn, dkn):
    b = x.shape[0]
    tm = TM_IN
    grid = (SEQ // tm, b)
    tok = lambda c: pl.BlockSpec((None, tm, c), lambda j, i: (i, j, 0))
    tr = lambda c: pl.BlockSpec((None, c, tm), lambda j, i: (i, 0, j))
    full = lambda a: pl.BlockSpec(a.shape, lambda j, i: (0,) * a.ndim)
    out_cols = [("tok", 256), ("tr", 256), ("tok", 256), ("tok", 256), ("tr", 256), ("tok", 512),
                ("tok", 512), ("tr", 512), ("tok", 512), ("tok", 256), ("tr", 128), ("tok", 256)]
    out_specs = [tok(c) if k == "tok" else tr(c) for k, c in out_cols]
    out_shape = [jax.ShapeDtypeStruct((b, SEQ, c) if k == "tok" else (b, c, SEQ), BF16) for k, c in out_cols]
    return pl.pallas_call(
        _inproj_kernel,
        grid=grid,
        in_specs=[tok(D_MODEL), full(g), full(w_mix),
                  pl.BlockSpec((9, tm, LANES), lambda j, i: (0, j, 0)),
                  full(cqn), full(ckvn), full(wuq), full(wukv), full(dqn), full(dkn)],
        out_specs=out_specs,
        out_shape=out_shape,
        compiler_params=pltpu.CompilerParams(
            dimension_semantics=("arbitrary", "arbitrary"), vmem_limit_bytes=VMEM_LIMIT),
        name="inproj",
    )(x, g, w_mix, tables, cqn, ckvn, wuq, wukv, dqn, dkn)


def _exp2_terms(s):
    return jnp.exp2(s - jnp.max(s, axis=-1, keepdims=True))


def _softmax_weights(job, s):
    return _exp2_terms(s).astype(BF16)


def _normalise_by_ones_lanes(o):
    return o * (1.0 / pltpu.roll(o, HEAD_DIM, 1))


def _join_halves(y_low, y_high):
    return jnp.where(_low_half(y_low.shape[0]), y_low, pltpu.roll(y_high, HEAD_DIM, 1))


def _pipeline(n, scores, weights, output):
    s = {0: scores(0)}
    w = {}
    for j in range(n + 1):
        if j + 1 < n:
            s[j + 1] = scores(j + 1)
        if j >= 1:
            output(j - 1, w.pop(j - 1))
        if j < n:
            w[j] = weights(j, s.pop(j))


def _tile_rows(i, t, tiles, tq):
    return pl.ds(pl.multiple_of((i * tiles + t) * tq, tq), tq)


def _attn_a_kernel(lam_ref, q_ref, kt_ref, v_ref, g_ref, o_ref, *, out_scale):
    lam = lam_ref[0]

    def tiles(i, carry):
        def scores(job):
            t, head = divmod(job, HEADS)
            return tuple(
                jnp.dot(q_ref[_tile_rows(i, t, TILES_DIFF, TQ_DIFF), A_D * j:A_D * (j + 1)], kt_ref[A_D * j:A_D * (j + 1), :],
                        preferred_element_type=F32) for j in (2 * head, 2 * head + 1))

        def weights(job, s):
            e0, e1 = _exp2_terms(s[0]), _exp2_terms(s[1])
            l0 = jnp.sum(e0, axis=-1, keepdims=True)
            l1 = jnp.sum(e1, axis=-1, keepdims=True)
            return (e0 - e1 * (lam * l0 / l1)).astype(BF16), l0

        held = {}

        def output(job, wl):
            t, head = divmod(job, HEADS)
            pair, half = divmod(head, 2)
            w, l0 = wl
            o = jnp.dot(w, v_ref[:, pair * LANES:(pair + 1) * LANES], preferred_element_type=F32) * (1.0 / l0)
            mask = _low_half(TQ_DIFF) if half == 0 else ~_low_half(TQ_DIFF)
            ms = jnp.sum(jnp.where(mask, o * o, 0.0), axis=-1, keepdims=True) * (1.0 / HEAD_DIM)
            y = o * lax.rsqrt(ms + EPS)
            if half == 0:
                held["low"] = y
                return
            y = jnp.where(_low_half(TQ_DIFF), held.pop("low"), y)
            o_ref[_tile_rows(i, t, TILES_DIFF, TQ_DIFF), pair * LANES:(pair + 1) * LANES] = (y * g_ref[...] * out_scale).astype(BF16)

        _pipeline(TILES_DIFF * HEADS, scores, weights, output)
        return carry

    lax.fori_loop(0, SEQ // (TQ_DIFF * TILES_DIFF), tiles, 0)


def _attn_c_kernel(q_ref, kt_ref, v_ref, o_ref):
    def tiles(i, carry):
        def scores(job):
            t, head = divmod(job, HEADS)
            cols = slice(head * LANES, (head + 1) * LANES)
            return jnp.dot(q_ref[_tile_rows(i, t, TILES_PLAIN, TQ_PLAIN), cols], kt_ref[cols, :], preferred_element_type=F32)

        held = {}

        def output(job, p):
            t, head = divmod(job, HEADS)
            pair, half = divmod(head, 2)
            y = _normalise_by_ones_lanes(
                jnp.dot(p, v_ref[:, head * LANES:(head + 1) * LANES], preferred_element_type=F32))
            if half == 0:
                held["low"] = y
                return
            o_ref[_tile_rows(i, t, TILES_PLAIN, TQ_PLAIN), pair * LANES:(pair + 1) * LANES] = _join_halves(held.pop("low"), y).astype(BF16)

        _pipeline(TILES_PLAIN * HEADS, scores, _softmax_weights, output)
        return carry

    lax.fori_loop(0, SEQ // (TQ_PLAIN * TILES_PLAIN), tiles, 0)


def _attn_d_kernel(q_ref, kt_ref, v_ref, o_ref):
    def tiles(i, carry):
        def scores(job):
            t, head = divmod(job, HEADS)
            kv_head = head // 2
            return jnp.dot(q_ref[_tile_rows(i, t, TILES_PLAIN, TQ_PLAIN), head * HEAD_DIM:(head + 1) * HEAD_DIM],
                           kt_ref[kv_head * HEAD_DIM:(kv_head + 1) * HEAD_DIM, :], preferred_element_type=F32)

        held = {}

        def output(job, p):
            t, head = divmod(job, HEADS)
            kv_head, g = divmod(head, 2)
            y = _normalise_by_ones_lanes(
                jnp.dot(p, v_ref[:, kv_head * LANES:(kv_head + 1) * LANES], preferred_element_type=F32))
            if g == 0:
                held["low"] = y
                return
            o_ref[_tile_rows(i, t, TILES_PLAIN, TQ_PLAIN), kv_head * LANES:(kv_head + 1) * LANES] = (
                _join_halves(held.pop("low"), y).astype(BF16))

        _pipeline(TILES_PLAIN * HEADS, scores, _softmax_weights, output)
        return carry

    lax.fori_loop(0, SEQ // (TQ_PLAIN * TILES_PLAIN), tiles, 0)


def _attn_full(kernel, q, kt, v, extra_in=(), extra_specs=(), scalar=None, name=None):
    b = q.shape[0]
    per_batch = lambda a: pl.BlockSpec((None,) + a.shape[1:], lambda i: (i, 0, 0))
    in_specs = [per_batch(q), per_batch(kt), per_batch(v)]
    args = [q, kt, v]
    if scalar is not None:
        in_specs = [pl.BlockSpec(memory_space=pltpu.SMEM)] + in_specs
        args = [scalar] + args
    in_specs += list(extra_specs)
    args += list(extra_in)
    return pl.pallas_call(
        kernel,
        grid=(b,),
        in_specs=in_specs,
        out_specs=pl.BlockSpec((None, SEQ, BRANCH_W), lambda i: (i, 0, 0)),
        out_shape=jax.ShapeDtypeStruct((b, SEQ, BRANCH_W), BF16),
        compiler_params=pltpu.CompilerParams(
            dimension_semantics=("arbitrary",), vmem_limit_bytes=VMEM_LIMIT),
        name=name,
    )(*args)


def _attn_b_kernel(q_ref, kt_ref, v_ref, bias_ref, o_ref):
    def steps(i, carry):
        def window(t):
            st = i * NB_STEPS_PER_ITER + t
            start = pl.multiple_of(jnp.clip(st - 2, 0, NB_STEPS - 5) * NB_Q, NB_Q)
            variant = jnp.minimum(st, 2) + jnp.maximum(st - (NB_STEPS - 3), 0)
            return pl.ds(pl.multiple_of(st * NB_Q, NB_Q), NB_Q), pl.ds(start, NB_KEYS), variant

        def scores(job):
            t, head = divmod(job, HEADS)
            rows, keys, variant = window(t)
            cols = slice(head * HEAD_DIM, (head + 1) * HEAD_DIM)
            s = jnp.dot(q_ref[rows, cols], kt_ref[cols, keys], preferred_element_type=F32)
            return s + bias_ref[variant, head]

        held = {}

        def output(job, p):
            t, head = divmod(job, HEADS)
            rows, keys, _ = window(t)
            pair, half = divmod(head, 2)
            y = _normalise_by_ones_lanes(
                jnp.dot(p, v_ref[keys, head * LANES:(head + 1) * LANES], preferred_element_type=F32))
            if half == 0:
                held["low"] = y
                return
            o_ref[rows, pair * LANES:(pair + 1) * LANES] = _join_halves(held.pop("low"), y).astype(BF16)

        _pipeline(NB_STEPS_PER_ITER * HEADS, scores, _softmax_weights, output)
        return carry

    lax.fori_loop(0, NB_STEPS // NB_STEPS_PER_ITER, steps, 0)


def _attn_b(q, k, v, bias):
    b = q.shape[0]
    per_batch = lambda a: pl.BlockSpec((None,) + a.shape[1:], lambda i: (i, 0, 0))
    return pl.pallas_call(
        _attn_b_kernel,
        grid=(b,),
        in_specs=[per_batch(q), per_batch(k), per_batch(v),
                  pl.BlockSpec(bias.shape, lambda i: (0, 0, 0, 0), pipeline_mode=pl.Buffered(1))],
        out_specs=pl.BlockSpec((None, SEQ, BRANCH_W), lambda i: (i, 0, 0)),
        out_shape=jax.ShapeDtypeStruct((b, SEQ, BRANCH_W), BF16),
        compiler_params=pltpu.CompilerParams(
            dimension_semantics=("arbitrary",), vmem_limit_bytes=VMEM_LIMIT),
        name="attn_b",
    )(q, k, v, bias)


def _nb_bias_tables(rpb):
    rows = SEQ // GRID_W
    steps = np.array([0, 1, 2, NB_STEPS - 2, NB_STEPS - 1])
    start_row = np.clip(steps - 2, 0, NB_STEPS - 5) * 2
    qr = np.arange(2)
    kr = np.arange(NB_KEYS // GRID_W)
    r = 2 * steps[:, None, None] + qr[None, :, None]
    krow = start_row[:, None, None] + kr[None, None, :]
    rs = np.clip(r - WIN_R // 2, 0, rows - WIN_R)
    row_ok = (krow >= rs) & (krow < rs + WIN_R)
    dr = np.clip(krow - r + WIN_R - 1, 0, 2 * WIN_R - 2)
    qc = np.arange(GRID_W)[:, None]
    kc = np.arange(GRID_W)[None, :]
    cs = np.clip(qc - WIN_C // 2, 0, GRID_W - WIN_C)
    col_ok = (kc >= cs) & (kc < cs + WIN_C)
    dc = np.clip(kc - qc + WIN_C - 1, 0, 2 * WIN_C - 2)
    row_sel = np.eye(2 * WIN_R - 1, dtype=np.float32)[dr.reshape(-1)]
    col_sel = np.eye(2 * WIN_C - 1, dtype=np.float32)[dc.reshape(-1)].T
    hp = lax.Precision.HIGHEST
    t = jnp.einsum("ar,hrc->hac", row_sel, rpb.astype(F32), precision=hp)
    t = jnp.einsum("hac,ck->hak", t, col_sel, precision=hp)
    nkr = NB_KEYS // GRID_W
    t = t.reshape(HEADS, NB_VARIANTS, 2, nkr, GRID_W, GRID_W)
    t = jnp.transpose(t, (1, 0, 2, 4, 3, 5)).reshape(NB_VARIANTS, HEADS, NB_Q, NB_KEYS)
    valid = row_ok[:, :, None, :, None] & col_ok[None, None, :, None, :]
    valid = valid.reshape(NB_VARIANTS, 1, NB_Q, NB_KEYS)
    return jnp.where(jnp.asarray(valid), t * LOG2E, NEG)


def _merge_kernel(x_ref, g_ref, oa_ref, ob_ref, oc_ref, od_ref, wg_ref, wb_ref, wo_ref, y_ref):
    x = x_ref[...]
    h = _rms(x, g_ref[...]).astype(BF16)
    merged = None
    for i, o_ref in enumerate((oa_ref, ob_ref, oc_ref, od_ref)):
        gate = jax.nn.sigmoid(jnp.dot(h, wg_ref[:, i * D_MODEL:(i + 1) * D_MODEL], preferred_element_type=F32))
        term = gate * jnp.dot(o_ref[...], wb_ref[i], preferred_element_type=F32)
        merged = term if merged is None else merged + term
    y_ref[...] = x + jnp.dot(merged.astype(BF16), wo_ref[...], preferred_element_type=F32)


def _merge(x, g, oa, ob, oc, od, wg, wb, wo):
    t = x.shape[0]
    tm = TM_MERGE
    tok = lambda c: pl.BlockSpec((tm, c), lambda i: (i, 0))
    full = lambda a: pl.BlockSpec(a.shape, lambda i: (0,) * a.ndim)
    return pl.pallas_call(
        _merge_kernel,
        grid=(t // tm,),
        in_specs=[tok(D_MODEL), full(g), tok(BRANCH_W), tok(BRANCH_W), tok(BRANCH_W), tok(BRANCH_W),
                  full(wg), full(wb), full(wo)],
        out_specs=tok(D_MODEL),
        out_shape=jax.ShapeDtypeStruct((t, D_MODEL), F32),
        compiler_params=pltpu.CompilerParams(
            dimension_semantics=("arbitrary",), vmem_limit_bytes=VMEM_LIMIT),
        name="merge",
    )(x, g, oa, ob, oc, od, wg, wb, wo)


def _ffn_kernel(xp_ref, x_ref, xn_ref, g_ref, win_ref, cw_ref, cb_ref, wo_ref, fg_ref, y_ref, act_scr, *, final):
    j = pl.program_id(1)
    tm = x_ref.shape[0]
    ext = tm + 2 * SUBLANES
    x = x_ref[...]
    h = _rms(jnp.concatenate([xp_ref[...], x, xn_ref[...]], axis=0), g_ref[...]).astype(BF16)
    keep_prev = (j > 0).astype(F32)
    keep_next = (j < pl.num_programs(1) - 1).astype(F32)
    row = lax.broadcasted_iota(jnp.int32, (ext, 1), 0)
    halo_scale = jnp.where(row < SUBLANES, keep_prev, jnp.where(row >= tm + SUBLANES, keep_next, 1.0))
    for c in range(N_FFN_CHUNKS):
        cols = slice(c * FFN_CHUNK, (c + 1) * FFN_CHUNK)
        u = jnp.dot(h, win_ref[:, cols], preferred_element_type=F32)
        g = jnp.dot(h, win_ref[:, FFN_DIM + c * FFN_CHUNK:FFN_DIM + (c + 1) * FFN_CHUNK],
                    preferred_element_type=F32) * halo_scale
        gc = (pltpu.roll(g, 1, 0) * cw_ref[0:1, cols] + g * cw_ref[1:2, cols]
              + pltpu.roll(g, ext - 1, 0) * cw_ref[2:3, cols] + cb_ref[:, cols])
        act = jax.nn.silu(gc) * u
        act_scr[:, cols] = act[SUBLANES:SUBLANES + tm].astype(BF16)
    y = x + jnp.dot(act_scr[...], wo_ref[...], preferred_element_type=F32)
    if final:
        y = _rms(y, fg_ref[...])
    y_ref[...] = y


def _ffn(x, g, w_in, conv_w, conv_b, w_out, final_g, final):
    b = x.shape[0]
    tm = TM_FFN
    per = tm // SUBLANES
    last = SEQ // SUBLANES - 1
    const = lambda a: pl.BlockSpec(a.shape, lambda i, j: (0,) * a.ndim, pipeline_mode=pl.Buffered(1))
    return pl.pallas_call(
        functools.partial(_ffn_kernel, final=final),
        grid=(b, SEQ // tm),
        in_specs=[pl.BlockSpec((None, SUBLANES, D_MODEL), lambda i, j: (i, jnp.maximum(j * per - 1, 0), 0)),
                  pl.BlockSpec((None, tm, D_MODEL), lambda i, j: (i, j, 0)),
                  pl.BlockSpec((None, SUBLANES, D_MODEL), lambda i, j: (i, jnp.minimum((j + 1) * per, last), 0)),
                  const(g), const(w_in), const(conv_w), const(conv_b), const(w_out), const(final_g)],
        out_specs=pl.BlockSpec((None, tm, D_MODEL), lambda i, j: (i, j, 0)),
        out_shape=jax.ShapeDtypeStruct((b, SEQ, D_MODEL), F32),
        scratch_shapes=[pltpu.VMEM((tm, FFN_DIM), BF16)],
        compiler_params=pltpu.CompilerParams(
            dimension_semantics=("arbitrary", "arbitrary"), vmem_limit_bytes=VMEM_LIMIT),
        name="ffn",
    )(x, x, x, g, w_in, conv_w, conv_b, w_out, final_g)


def _rope_lane_tables():
    t = jnp.arange(SEQ)
    inv_freq = ROPE_THETA ** (-jnp.arange(0, 2 * ROPE_HALF, 2, dtype=F32) / (2 * ROPE_HALF))
    ang = lambda pos: pos.astype(F32)[:, None] * inv_freq[None, :]
    lane = np.arange(LANES)
    freq = lane % ROPE_HALF
    first = jnp.asarray((lane % (2 * ROPE_HALF)) < ROPE_HALF)[None, :]
    ang_seq = ang(t)[:, freq]
    ang_axial = jnp.where(jnp.asarray((lane // (2 * ROPE_HALF)) % 2 == 0)[None, :],
                          ang(t // GRID_W)[:, freq], ang(t % GRID_W)[:, freq])
    active = jnp.asarray((lane >= C_NOPE) & (lane < C_NOPE + C_ROPE))[None, :]

    def trio(a, on):
        cos, sin = jnp.cos(a), jnp.sin(a)
        return [jnp.where(on, cos, 1.0), jnp.where(on & first, -sin, 0.0), jnp.where(on & ~first, sin, 0.0)]

    everywhere = jnp.ones((1, LANES), bool)
    return jnp.stack(trio(ang_seq, everywhere) + trio(ang_seq, active) + trio(ang_axial, everywhere))


def _pack_layer(l, w_in, c_q_norm, c_kv_norm, c_w_uq, c_w_ukv, d_q_norm, d_k_norm):
    w = w_in[l]
    a, bb, cc, dd, gate = jnp.split(w, [768, 1536, 1888, 2400], axis=1)
    z = lambda n: jnp.zeros((D_MODEL, n), F32)
    cq, ckv, krope = cc[:, :C_QLORA], cc[:, C_QLORA:C_QLORA + C_KVLORA], cc[:, C_QLORA + C_KVLORA:]
    w_mix = jnp.concatenate([a, bb, dd, cq, z(64), ckv, z(C_NOPE), krope, z(LANES - C_NOPE - C_ROPE)], axis=1)
    assert w_mix.shape[1] == MIX_COLS
    wuq = c_w_uq[l].reshape(C_QLORA, HEADS, C_NOPE + C_ROPE)
    wuq = jnp.pad(wuq, ((0, 256 - C_QLORA), (0, 0), (0, LANES - C_NOPE - C_ROPE))).reshape(256, HEADS * LANES)
    wukv = c_w_ukv[l].reshape(C_KVLORA, HEADS, C_NOPE + HEAD_DIM)
    halves = ((0, 0), (0, 0), (0, LANES - HEAD_DIM))
    wk = jnp.pad(wukv[:, :, :C_NOPE], halves).reshape(C_KVLORA, HEADS * LANES)
    wv = jnp.pad(wukv[:, :, C_NOPE:], halves).reshape(C_KVLORA, HEADS * LANES)
    wukv = jnp.concatenate([wk, wv], axis=1)
    cqn = jnp.pad(c_q_norm[l], (0, 256 - C_QLORA))[None, :]
    return dict(
        w_mix=w_mix.astype(BF16), w_gate=gate.astype(BF16), wuq=wuq.astype(BF16), wukv=wukv.astype(BF16),
        cqn=cqn, ckvn=c_kv_norm[l][None, :],
        dqn=jnp.tile(d_q_norm[l], 2)[None, :], dkn=jnp.tile(d_k_norm[l], 2)[None, :])


def _trunk(x, tables, layers, final_norm):
    b = x.shape[0]
    for l, p in enumerate(layers):
        (qa, kat, va, qb, kb, vb, qc, kct, vc, qd, kdt, vd) = _inproj(
            x, p["attn_norm"], p["w_mix"], tables, p["cqn"], p["ckvn"], p["wuq"], p["wukv"], p["dqn"], p["dkn"])
        lam_init = 0.8 - 0.6 * math.exp(-0.3 * l)
        oa = _attn_full(functools.partial(_attn_a_kernel, out_scale=1.0 - lam_init), qa, kat, va,
                        extra_in=(p["subln"],), extra_specs=(pl.BlockSpec((1, LANES), lambda i: (0, 0)),),
                        scalar=p["lam"], name="attn_a")
        ob = _attn_b(qb, kb, vb, p["nb_bias"])
        oc = _attn_full(_attn_c_kernel, qc, kct, vc, name="attn_c")
        od = _attn_full(_attn_d_kernel, qd, kdt, vd, name="attn_d")
        flat = lambda a: a.reshape(b * SEQ, a.shape[-1])
        x1 = _merge(flat(x), p["attn_norm"], flat(oa), flat(ob), flat(oc), flat(od),
                    p["w_gate"], p["w_branch"], p["w_out"]).reshape(b, SEQ, D_MODEL)
        x = _ffn(x1, p["ffn_norm"], p["w_ffn_in"], p["conv_w"], p["conv_b"], p["w_ffn_out"],
                 final_norm, final=(l == DEPTH - 1))
    return x


def kernel(x_prompt, x_sample, attn_norm, w_in, a_lambda_q1, a_lambda_k1, a_lambda_q2, a_lambda_k2, a_subln, b_rpb, c_q_norm, c_kv_norm, c_w_uq, c_w_ukv, d_q_norm, d_k_norm, w_branch, w_out, ffn_norm, w_ffn_in, ffn_conv_w, ffn_conv_b, w_ffn_out, final_norm):
    tables = _rope_lane_tables()
    layers = []
    for l in range(DEPTH):
        p = _pack_layer(l, w_in, c_q_norm, c_kv_norm, c_w_uq, c_w_ukv, d_q_norm, d_k_norm)
        lam_init = 0.8 - 0.6 * math.exp(-0.3 * l)
        lam = (jnp.exp(jnp.sum(a_lambda_q1[l] * a_lambda_k1[l]))
               - jnp.exp(jnp.sum(a_lambda_q2[l] * a_lambda_k2[l])) + lam_init)
        p.update(
            attn_norm=attn_norm[l][None, :], lam=lam.reshape(1).astype(F32),
            subln=jnp.tile(a_subln[l], 2)[None, :], nb_bias=_nb_bias_tables(b_rpb[l]),
            w_branch=w_branch[l].astype(BF16), w_out=w_out[l].astype(BF16),
            ffn_norm=ffn_norm[l][None, :], w_ffn_in=w_ffn_in[l].astype(BF16),
            conv_w=ffn_conv_w[l], conv_b=ffn_conv_b[l][None, :], w_ffn_out=w_ffn_out[l].astype(BF16))
        layers.append(p)
    fn = final_norm[None, :]
    return (_trunk(x_prompt, tables, layers, fn), _trunk(x_sample, tables, layers, fn))
```

```python
import functools
import math

import numpy as np
import jax
import jax.numpy as jnp
from jax import lax
from jax.experimental import pallas as pl
from jax.experimental.pallas import tpu as pltpu

F32 = jnp.float32
BF16 = jnp.bfloat16

D_MODEL = 1024
SEQ = 2048
DEPTH = 2
GRID_W = 64
ROPE_THETA = 10000.0
EPS = 1e-6
HEADS = 4
HEAD_DIM = 64
A_D = 32
WIN_R = 8
WIN_C = 16
C_NOPE = 64
C_ROPE = 32
C_QLORA = 192
C_KVLORA = 128
D_KV_HEADS = 2
BRANCH_W = 256
FFN_DIM = 2816
LANES = 128
SUBLANES = 8
ROPE_HALF = 16
LOG2E = 1.4426950408889634

MIX_COLS = 2560
TM_IN = 1024
TM_MERGE = 1024
TQ_DIFF = 512
TILES_DIFF = 1
TQ_PLAIN = 1024
TILES_PLAIN = 2
NB_STEPS_PER_ITER = 16
NB_Q = 2 * GRID_W
NB_KEYS = 10 * GRID_W
NB_STEPS = SEQ // NB_Q
NB_VARIANTS = 5
NEG = -1e30
TM_FFN = 1024
FFN_CHUNK = 256
N_FFN_CHUNKS = FFN_DIM // FFN_CHUNK
VMEM_LIMIT = 56 * 1024 * 1024


def _rms(x, g):
    return x * lax.rsqrt(jnp.mean(x * x, axis=-1, keepdims=True) + EPS) * g


def _rope_slab(x, cos, sin_lo, sin_hi):
    return (x * cos + pltpu.roll(x, LANES - ROPE_HALF, 1) * sin_lo
            + pltpu.roll(x, ROPE_HALF, 1) * sin_hi)


def _group_mean_sq(x, width):
    sq = x * x
    hi = sq.astype(BF16)
    lo = (sq - hi.astype(F32)).astype(BF16)
    r = lax.broadcasted_iota(jnp.int32, (LANES, LANES), 0) // width
    c = lax.broadcasted_iota(jnp.int32, (LANES, LANES), 1) // width
    ones = jnp.where(r == c, 1.0, 0.0).astype(BF16)
    tot = (jnp.dot(hi, ones, preferred_element_type=F32)
           + jnp.dot(lo, ones, preferred_element_type=F32))
    return tot * (1.0 / width)


def _low_half(rows):
    return lax.broadcasted_iota(jnp.int32, (rows, LANES), 1) < HEAD_DIM


def _value_with_ones(v, shift):
    if shift:
        v = pltpu.roll(v, HEAD_DIM, 1)
    return jnp.where(_low_half(v.shape[0]), v, 1.0).astype(BF16)


def _inproj_kernel(x_ref, g_ref, w_ref, tab_ref, cqn_ref, ckvn_ref, wuq_ref, wukv_ref, dqn_ref, dkn_ref,
                   qa_ref, kat_ref, va_ref, qb_ref, kbt_ref, vb_ref,
                   qc_ref, kct_ref, vc_ref, qd_ref, kdt_ref, vd_ref):
    h = _rms(x_ref[...], g_ref[...]).astype(BF16)

    def project(first_slab, n_slabs):
        cols = slice(first_slab * LANES, (first_slab + n_slabs) * LANES)
        p = jnp.dot(h, w_ref[:, cols], preferred_element_type=F32)
        return lambda i: p[:, (i - first_slab) * LANES:(i - first_slab + 1) * LANES]

    def rope(x, variant):
        return _rope_slab(x, tab_ref[3 * variant], tab_ref[3 * variant + 1], tab_ref[3 * variant + 2])

    def store_slabs(ref, slabs):
        for i, v in enumerate(slabs):
            ref[:, i * LANES:(i + 1) * LANES] = v.astype(BF16)

    c_slab = project(16, 4)
    a_slab = project(0, 6)
    cq = jnp.concatenate([c_slab(16), c_slab(17)], axis=1)
    cq = cq * lax.rsqrt(jnp.sum(cq * cq, axis=-1, keepdims=True) * (1.0 / C_QLORA) + EPS) * cqn_ref[...]
    q = jnp.dot(cq.astype(BF16), wuq_ref[...], preferred_element_type=F32)
    ckv = _rms(c_slab(18), ckvn_ref[...])
    kv = jnp.dot(ckv.astype(BF16), wukv_ref[...], preferred_element_type=F32)
    b_slab = project(6, 6)
    store_slabs(qa_ref, [rope(a_slab(i), 0) * (A_D ** -0.5 * LOG2E) for i in range(2)])
    for i in range(2):
        kat_ref[i * LANES:(i + 1) * LANES, :] = rope(a_slab(2 + i), 0).T.astype(BF16)
    store_slabs(va_ref, [a_slab(4), a_slab(5)])
    d_slab = project(12, 4)
    store_slabs(qb_ref, [b_slab(6 + i) * (HEAD_DIM ** -0.5 * LOG2E) for i in range(2)])
    for i in range(2):
        kbt_ref[i * LANES:(i + 1) * LANES, :] = b_slab(8 + i).T.astype(BF16)
    store_slabs(vb_ref, [_value_with_ones(b_slab(10 + i // 2), i % 2) for i in range(HEADS)])
    kr = rope(c_slab(19), 1)
    store_slabs(qc_ref, [rope(q[:, i * LANES:(i + 1) * LANES], 1) * ((C_NOPE + C_ROPE) ** -0.5 * LOG2E)
                         for i in range(HEADS)])
    for i in range(HEADS):
        kct_ref[i * LANES:(i + 1) * LANES, :] = (kv[:, i * LANES:(i + 1) * LANES] + kr).T.astype(BF16)
    store_slabs(vc_ref, [_value_with_ones(kv[:, (HEADS + i) * LANES:(HEADS + i + 1) * LANES], 0)
                         for i in range(HEADS)])
    qd = []
    for i in range(2):
        x = d_slab(12 + i)
        x = x * lax.rsqrt(_group_mean_sq(x, HEAD_DIM) + EPS) * dqn_ref[...]
        qd.append(rope(x, 2) * (HEAD_DIM ** -0.5 * LOG2E))
    store_slabs(qd_ref, qd)
    x = d_slab(14)
    x = x * lax.rsqrt(_group_mean_sq(x, HEAD_DIM) + EPS) * dkn_ref[...]
    kdt_ref[...] = rope(x, 2).T.astype(BF16)
    store_slabs(vd_ref, [_value_with_ones(d_slab(15), i) for i in range(D_KV_HEADS)])


def _inproj(x, g, w_mix, tables, cqn, ckvn, wuq, wukv, dqn, dkn):
    b = x.shape[0]
    tm = TM_IN
    grid = (SEQ // tm, b)
    tok = lambda c: pl.BlockSpec((None, tm, c), lambda j, i: (i, j, 0))
    tr = lambda c: pl.BlockSpec((None, c, tm), lambda j, i: (i, 0, j))
    full = lambda a: pl.BlockSpec(a.shape, lambda j, i: (0,) * a.ndim)
    out_cols = [("tok", 256), ("tr", 256), ("tok", 256), ("tok", 256), ("tr", 256), ("tok", 512),
                ("tok", 512), ("tr", 512), ("tok", 512), ("tok", 256), ("tr", 128), ("tok", 256)]
    out_specs = [tok(c) if k == "tok" else tr(c) for k, c in out_cols]
    out_shape = [jax.ShapeDtypeStruct((b, SEQ, c) if k == "tok" else (b, c, SEQ), BF16) for k, c in out_cols]
    return pl.pallas_call(
        _inproj_kernel,
        grid=grid,
        in_specs=[tok(D_MODEL), full(g), full(w_mix),
                  pl.BlockSpec((9, tm, LANES), lambda j, i: (0, j, 0)),
                  full(cqn), full(ckvn), full(wuq), full(wukv), full(dqn), full(dkn)],
        out_specs=out_specs,
        out_shape=out_shape,
        compiler_params=pltpu.CompilerParams(
            dimension_semantics=("arbitrary", "arbitrary"), vmem_limit_bytes=VMEM_LIMIT),
        name="inproj",
    )(x, g, w_mix, tables, cqn, ckvn, wuq, wukv, dqn, dkn)


def _exp2_terms(s):
    return jnp.exp2(s - jnp.max(s, axis=-1, keepdims=True))


def _softmax_weights(job, s):
    return _exp2_terms(s).astype(BF16)


def _normalise_by_ones_lanes(o):
    return o * (1.0 / pltpu.roll(o, HEAD_DIM, 1))


def _join_halves(y_low, y_high):
    return jnp.where(_low_half(y_low.shape[0]), y_low, pltpu.roll(y_high, HEAD_DIM, 1))


def _pipeline(n, scores, weights, output):
    s = {0: scores(0)}
    w = {}
    for j in range(n + 1):
        if j + 1 < n:
            s[j + 1] = scores(j + 1)
        if j >= 1:
            output(j - 1, w.pop(j - 1))
        if j < n:
            w[j] = weights(j, s.pop(j))


def _tile_rows(i, t, tiles, tq):
    return pl.ds(pl.multiple_of((i * tiles + t) * tq, tq), tq)


def _attn_a_kernel(lam_ref, q_ref, kt_ref, v_ref, g_ref, o_ref, *, out_scale):
    lam = lam_ref[0]

    def tiles(i, carry):
        def scores(job):
            t, head = divmod(job, HEADS)
            return tuple(
                jnp.dot(q_ref[_tile_rows(i, t, TILES_DIFF, TQ_DIFF), A_D * j:A_D * (j + 1)], kt_ref[A_D * j:A_D * (j + 1), :],
                        preferred_element_type=F32) for j in (2 * head, 2 * head + 1))

        def weights(job, s):
            e0, e1 = _exp2_terms(s[0]), _exp2_terms(s[1])
            l0 = jnp.sum(e0, axis=-1, keepdims=True)
            l1 = jnp.sum(e1, axis=-1, keepdims=True)
            return (e0 - e1 * (lam * l0 / l1)).astype(BF16), l0

        held = {}

        def output(job, wl):
            t, head = divmod(job, HEADS)
            pair, half = divmod(head, 2)
            w, l0 = wl
            o = jnp.dot(w, v_ref[:, pair * LANES:(pair + 1) * LANES], preferred_element_type=F32) * (1.0 / l0)
            mask = _low_half(TQ_DIFF) if half == 0 else ~_low_half(TQ_DIFF)
            ms = jnp.sum(jnp.where(mask, o * o, 0.0), axis=-1, keepdims=True) * (1.0 / HEAD_DIM)
            y = o * lax.rsqrt(ms + EPS)
            if half == 0:
                held["low"] = y
                return
            y = jnp.where(_low_half(TQ_DIFF), held.pop("low"), y)
            o_ref[_tile_rows(i, t, TILES_DIFF, TQ_DIFF), pair * LANES:(pair + 1) * LANES] = (y * g_ref[...] * out_scale).astype(BF16)

        _pipeline(TILES_DIFF * HEADS, scores, weights, output)
        return carry

    lax.fori_loop(0, SEQ // (TQ_DIFF * TILES_DIFF), tiles, 0)


def _attn_c_kernel(q_ref, kt_ref, v_ref, o_ref):
    def tiles(i, carry):
        def scores(job):
            t, head = divmod(job, HEADS)
            cols = slice(head * LANES, (head + 1) * LANES)
            return jnp.dot(q_ref[_tile_rows(i, t, TILES_PLAIN, TQ_PLAIN), cols], kt_ref[cols, :], preferred_element_type=F32)

        held = {}

        def output(job, p):
            t, head = divmod(job, HEADS)
            pair, half = divmod(head, 2)
            y = _normalise_by_ones_lanes(
                jnp.dot(p, v_ref[:, head * LANES:(head + 1) * LANES], preferred_element_type=F32))
            if half == 0:
                held["low"] = y
                return
            o_ref[_tile_rows(i, t, TILES_PLAIN, TQ_PLAIN), pair * LANES:(pair + 1) * LANES] = _join_halves(held.pop("low"), y).astype(BF16)

        _pipeline(TILES_PLAIN * HEADS, scores, _softmax_weights, output)
        return carry

    lax.fori_loop(0, SEQ // (TQ_PLAIN * TILES_PLAIN), tiles, 0)


def _attn_d_kernel(q_ref, kt_ref, v_ref, o_ref):
    def tiles(i, carry):
        def scores(job):
            t, head = divmod(job, HEADS)
            kv_head = head // 2
            return jnp.dot(q_ref[_tile_rows(i, t, TILES_PLAIN, TQ_PLAIN), head * HEAD_DIM:(head + 1) * HEAD_DIM],
                           kt_ref[kv_head * HEAD_DIM:(kv_head + 1) * HEAD_DIM, :], preferred_element_type=F32)

        held = {}

        def output(job, p):
            t, head = divmod(job, HEADS)
            kv_head, g = divmod(head, 2)
            y = _normalise_by_ones_lanes(
                jnp.dot(p, v_ref[:, kv_head * LANES:(kv_head + 1) * LANES], preferred_element_type=F32))
            if g == 0:
                held["low"] = y
                return
            o_ref[_tile_rows(i, t, TILES_PLAIN, TQ_PLAIN), kv_head * LANES:(kv_head + 1) * LANES] = (
                _join_halves(held.pop("low"), y).astype(BF16))

        _pipeline(TILES_PLAIN * HEADS, scores, _softmax_weights, output)
        return carry

    lax.fori_loop(0, SEQ // (TQ_PLAIN * TILES_PLAIN), tiles, 0)


def _attn_full(kernel, q, kt, v, extra_in=(), extra_specs=(), scalar=None, name=None):
    b = q.shape[0]
    per_batch = lambda a: pl.BlockSpec((None,) + a.shape[1:], lambda i: (i, 0, 0))
    in_specs = [per_batch(q), per_batch(kt), per_batch(v)]
    args = [q, kt, v]
    if scalar is not None:
        in_specs = [pl.BlockSpec(memory_space=pltpu.SMEM)] + in_specs
        args = [scalar] + args
    in_specs += list(extra_specs)
    args += list(extra_in)
    return pl.pallas_call(
        kernel,
        grid=(b,),
        in_specs=in_specs,
        out_specs=pl.BlockSpec((None, SEQ, BRANCH_W), lambda i: (i, 0, 0)),
        out_shape=jax.ShapeDtypeStruct((b, SEQ, BRANCH_W), BF16),
        compiler_params=pltpu.CompilerParams(
            dimension_semantics=("arbitrary",), vmem_limit_bytes=VMEM_LIMIT),
        name=name,
    )(*args)


def _attn_b_kernel(q_ref, kt_ref, v_ref, bias_ref, o_ref):
    def steps(i, carry):
        def window(t):
            st = i * NB_STEPS_PER_ITER + t
            start = pl.multiple_of(jnp.clip(st - 2, 0, NB_STEPS - 5) * NB_Q, NB_Q)
            variant = jnp.minimum(st, 2) + jnp.maximum(st - (NB_STEPS - 3), 0)
            return pl.ds(pl.multiple_of(st * NB_Q, NB_Q), NB_Q), pl.ds(start, NB_KEYS), variant

        def scores(job):
            t, head = divmod(job, HEADS)
            rows, keys, variant = window(t)
            cols = slice(head * HEAD_DIM, (head + 1) * HEAD_DIM)
            s = jnp.dot(q_ref[rows, cols], kt_ref[cols, keys], preferred_element_type=F32)
            return s + bias_ref[variant, head]

        held = {}

        def output(job, p):
            t, head = divmod(job, HEADS)
            rows, keys, _ = window(t)
            pair, half = divmod(head, 2)
            y = _normalise_by_ones_lanes(
                jnp.dot(p, v_ref[keys, head * LANES:(head + 1) * LANES], preferred_element_type=F32))
            if half == 0:
                held["low"] = y
                return
            o_ref[rows, pair * LANES:(pair + 1) * LANES] = _join_halves(held.pop("low"), y).astype(BF16)

        _pipeline(NB_STEPS_PER_ITER * HEADS, scores, _softmax_weights, output)
        return carry

    lax.fori_loop(0, NB_STEPS // NB_STEPS_PER_ITER, steps, 0)


def _attn_b(q, k, v, bias):
    b = q.shape[0]
    per_batch = lambda a: pl.BlockSpec((None,) + a.shape[1:], lambda i: (i, 0, 0))
    return pl.pallas_call(
        _attn_b_kernel,
        grid=(b,),
        in_specs=[per_batch(q), per_batch(k), per_batch(v),
                  pl.BlockSpec(bias.shape, lambda i: (0, 0, 0, 0), pipeline_mode=pl.Buffered(1))],
        out_specs=pl.BlockSpec((None, SEQ, BRANCH_W), lambda i: (i, 0, 0)),
        out_shape=jax.ShapeDtypeStruct((b, SEQ, BRANCH_W), BF16),
        compiler_params=pltpu.CompilerParams(
            dimension_semantics=("arbitrary",), vmem_limit_bytes=VMEM_LIMIT),
        name="attn_b",
    )(q, k, v, bias)


def _nb_bias_tables(rpb):
    rows = SEQ // GRID_W
    steps = np.array([0, 1, 2, NB_STEPS - 2, NB_STEPS - 1])
    start_row = np.clip(steps - 2, 0, NB_STEPS - 5) * 2
    qr = np.arange(2)
    kr = np.arange(NB_KEYS // GRID_W)
    r = 2 * steps[:, None, None] + qr[None, :, None]
    krow = start_row[:, None, None] + kr[None, None, :]
    rs = np.clip(r - WIN_R // 2, 0, rows - WIN_R)
    row_ok = (krow >= rs) & (krow < rs + WIN_R)
    dr = np.clip(krow - r + WIN_R - 1, 0, 2 * WIN_R - 2)
    qc = np.arange(GRID_W)[:, None]
    kc = np.arange(GRID_W)[None, :]
    cs = np.clip(qc - WIN_C // 2, 0, GRID_W - WIN_C)
    col_ok = (kc >= cs) & (kc < cs + WIN_C)
    dc = np.clip(kc - qc + WIN_C - 1, 0, 2 * WIN_C - 2)
    row_sel = np.eye(2 * WIN_R - 1, dtype=np.float32)[dr.reshape(-1)]
    col_sel = np.eye(2 * WIN_C - 1, dtype=np.float32)[dc.reshape(-1)].T
    hp = lax.Precision.HIGHEST
    t = jnp.einsum("ar,hrc->hac", row_sel, rpb.astype(F32), precision=hp)
    t = jnp.einsum("hac,ck->hak", t, col_sel, precision=hp)
    nkr = NB_KEYS // GRID_W
    t = t.reshape(HEADS, NB_VARIANTS, 2, nkr, GRID_W, GRID_W)
    t = jnp.transpose(t, (1, 0, 2, 4, 3, 5)).reshape(NB_VARIANTS, HEADS, NB_Q, NB_KEYS)
    valid = row_ok[:, :, None, :, None] & col_ok[None, None, :, None, :]
    valid = valid.reshape(NB_VARIANTS, 1, NB_Q, NB_KEYS)
    return jnp.where(jnp.asarray(valid), t * LOG2E, NEG)


def _merge_kernel(x_ref, g_ref, oa_ref, ob_ref, oc_ref, od_ref, wg_ref, wb_ref, wo_ref, y_ref):
    x = x_ref[...]
    h = _rms(x, g_ref[...]).astype(BF16)
    merged = None
    for i, o_ref in enumerate((oa_ref, ob_ref, oc_ref, od_ref)):
        gate = jax.nn.sigmoid(jnp.dot(h, wg_ref[:, i * D_MODEL:(i + 1) * D_MODEL], preferred_element_type=F32))
        term = gate * jnp.dot(o_ref[...], wb_ref[i], preferred_element_type=F32)
        merged = term if merged is None else merged + term
    y_ref[...] = x + jnp.dot(merged.astype(BF16), wo_ref[...], preferred_element_type=F32)


def _merge(x, g, oa, ob, oc, od, wg, wb, wo):
    t = x.shape[0]
    tm = TM_MERGE
    tok = lambda c: pl.BlockSpec((tm, c), lambda i: (i, 0))
    full = lambda a: pl.BlockSpec(a.shape, lambda i: (0,) * a.ndim, pipeline_mode=pl.Buffered(1))
    return pl.pallas_call(
        _merge_kernel,
        grid=(t // tm,),
        in_specs=[tok(D_MODEL), full(g), tok(BRANCH_W), tok(BRANCH_W), tok(BRANCH_W), tok(BRANCH_W),
                  full(wg), full(wb), full(wo)],
        out_specs=tok(D_MODEL),
        out_shape=jax.ShapeDtypeStruct((t, D_MODEL), F32),
        compiler_params=pltpu.CompilerParams(
            dimension_semantics=("arbitrary",), vmem_limit_bytes=VMEM_LIMIT),
        name="merge",
    )(x, g, oa, ob, oc, od, wg, wb, wo)


def _ffn_kernel(xp_ref, x_ref, xn_ref, g_ref, win_ref, cw_ref, cb_ref, wo_ref, fg_ref, y_ref, act_scr, *, final):
    j = pl.program_id(1)
    tm = x_ref.shape[0]
    ext = tm + 2 * SUBLANES
    x = x_ref[...]
    h = _rms(jnp.concatenate([xp_ref[...], x, xn_ref[...]], axis=0), g_ref[...]).astype(BF16)
    keep_prev = (j > 0).astype(F32)
    keep_next = (j < pl.num_programs(1) - 1).astype(F32)
    row = lax.broadcasted_iota(jnp.int32, (ext, 1), 0)
    halo_scale = jnp.where(row < SUBLANES, keep_prev, jnp.where(row >= tm + SUBLANES, keep_next, 1.0))
    for c in range(N_FFN_CHUNKS):
        cols = slice(c * FFN_CHUNK, (c + 1) * FFN_CHUNK)
        u = jnp.dot(h, win_ref[:, cols], preferred_element_type=F32)
        g = jnp.dot(h, win_ref[:, FFN_DIM + c * FFN_CHUNK:FFN_DIM + (c + 1) * FFN_CHUNK],
                    preferred_element_type=F32) * halo_scale
        gc = (pltpu.roll(g, 1, 0) * cw_ref[0:1, cols] + g * cw_ref[1:2, cols]
              + pltpu.roll(g, ext - 1, 0) * cw_ref[2:3, cols] + cb_ref[:, cols])
        act = jax.nn.silu(gc) * u
        act_scr[:, cols] = act[SUBLANES:SUBLANES + tm].astype(BF16)
    y = x + jnp.dot(act_scr[...], wo_ref[...], preferred_element_type=F32)
    if final:
        y = _rms(y, fg_ref[...])
    y_ref[...] = y


def _ffn(x, g, w_in, conv_w, conv_b, w_out, final_g, final):
    b = x.shape[0]
    tm = TM_FFN
    per = tm // SUBLANES
    last = SEQ // SUBLANES - 1
    const = lambda a: pl.BlockSpec(a.shape, lambda i, j: (0,) * a.ndim, pipeline_mode=pl.Buffered(1))
    return pl.pallas_call(
        functools.partial(_ffn_kernel, final=final),
        grid=(b, SEQ // tm),
        in_specs=[pl.BlockSpec((None, SUBLANES, D_MODEL), lambda i, j: (i, jnp.maximum(j * per - 1, 0), 0)),
                  pl.BlockSpec((None, tm, D_MODEL), lambda i, j: (i, j, 0)),
                  pl.BlockSpec((None, SUBLANES, D_MODEL), lambda i, j: (i, jnp.minimum((j + 1) * per, last), 0)),
                  const(g), const(w_in), const(conv_w), const(conv_b), const(w_out), const(final_g)],
        out_specs=pl.BlockSpec((None, tm, D_MODEL), lambda i, j: (i, j, 0)),
        out_shape=jax.ShapeDtypeStruct((b, SEQ, D_MODEL), F32),
        scratch_shapes=[pltpu.VMEM((tm, FFN_DIM), BF16)],
        compiler_params=pltpu.CompilerParams(
            dimension_semantics=("arbitrary", "arbitrary"), vmem_limit_bytes=VMEM_LIMIT),
        name="ffn",
    )(x, x, x, g, w_in, conv_w, conv_b, w_out, final_g)


def _rope_lane_tables():
    t = jnp.arange(SEQ)
    inv_freq = ROPE_THETA ** (-jnp.arange(0, 2 * ROPE_HALF, 2, dtype=F32) / (2 * ROPE_HALF))
    ang = lambda pos: pos.astype(F32)[:, None] * inv_freq[None, :]
    lane = np.arange(LANES)
    freq = lane % ROPE_HALF
    first = jnp.asarray((lane % (2 * ROPE_HALF)) < ROPE_HALF)[None, :]
    ang_seq = ang(t)[:, freq]
    ang_axial = jnp.where(jnp.asarray((lane // (2 * ROPE_HALF)) % 2 == 0)[None, :],
                          ang(t // GRID_W)[:, freq], ang(t % GRID_W)[:, freq])
    active = jnp.asarray((lane >= C_NOPE) & (lane < C_NOPE + C_ROPE))[None, :]

    def trio(a, on):
        cos, sin = jnp.cos(a), jnp.sin(a)
        return [jnp.where(on, cos, 1.0), jnp.where(on & first, -sin, 0.0), jnp.where(on & ~first, sin, 0.0)]

    everywhere = jnp.ones((1, LANES), bool)
    return jnp.stack(trio(ang_seq, everywhere) + trio(ang_seq, active) + trio(ang_axial, everywhere))


def _pack_layer(l, w_in, c_q_norm, c_kv_norm, c_w_uq, c_w_ukv, d_q_norm, d_k_norm):
    w = w_in[l]
    a, bb, cc, dd, gate = jnp.split(w, [768, 1536, 1888, 2400], axis=1)
    z = lambda n: jnp.zeros((D_MODEL, n), F32)
    cq, ckv, krope = cc[:, :C_QLORA], cc[:, C_QLORA:C_QLORA + C_KVLORA], cc[:, C_QLORA + C_KVLORA:]
    w_mix = jnp.concatenate([a, bb, dd, cq, z(64), ckv, z(C_NOPE), krope, z(LANES - C_NOPE - C_ROPE)], axis=1)
    assert w_mix.shape[1] == MIX_COLS
    wuq = c_w_uq[l].reshape(C_QLORA, HEADS, C_NOPE + C_ROPE)
    wuq = jnp.pad(wuq, ((0, 256 - C_QLORA), (0, 0), (0, LANES - C_NOPE - C_ROPE))).reshape(256, HEADS * LANES)
    wukv = c_w_ukv[l].reshape(C_KVLORA, HEADS, C_NOPE + HEAD_DIM)
    halves = ((0, 0), (0, 0), (0, LANES - HEAD_DIM))
    wk = jnp.pad(wukv[:, :, :C_NOPE], halves).reshape(C_KVLORA, HEADS * LANES)
    wv = jnp.pad(wukv[:, :, C_NOPE:], halves).reshape(C_KVLORA, HEADS * LANES)
    wukv = jnp.concatenate([wk, wv], axis=1)
    cqn = jnp.pad(c_q_norm[l], (0, 256 - C_QLORA))[None, :]
    return dict(
        w_mix=w_mix.astype(BF16), w_gate=gate.astype(BF16), wuq=wuq.astype(BF16), wukv=wukv.astype(BF16),
        cqn=cqn, ckvn=c_kv_norm[l][None, :],
        dqn=jnp.tile(d_q_norm[l], 2)[None, :], dkn=jnp.tile(d_k_norm[l], 2)[None, :])


def _trunk(x, tables, layers, final_norm):
    b = x.shape[0]
    for l, p in enumerate(layers):
        (qa, kat, va, qb, kb, vb, qc, kct, vc, qd, kdt, vd) = _inproj(
            x, p["attn_norm"], p["w_mix"], tables, p["cqn"], p["ckvn"], p["wuq"], p["wukv"], p["dqn"], p["dkn"])
        lam_init = 0.8 - 0.6 * math.exp(-0.3 * l)
        oa = _attn_full(functools.partial(_attn_a_kernel, out_scale=1.0 - lam_init), qa, kat, va,
                        extra_in=(p["subln"],), extra_specs=(pl.BlockSpec((1, LANES), lambda i: (0, 0)),),
                        scalar=p["lam"], name="attn_a")
        ob = _attn_b(qb, kb, vb, p["nb_bias"])
        oc = _attn_full(_attn_c_kernel, qc, kct, vc, name="attn_c")
        od = _attn_full(_attn_d_kernel, qd, kdt, vd, name="attn_d")
        flat = lambda a: a.reshape(b * SEQ, a.shape[-1])
        x1 = _merge(flat(x), p["attn_norm"], flat(oa), flat(ob), flat(oc), flat(od),
                    p["w_gate"], p["w_branch"], p["w_out"]).reshape(b, SEQ, D_MODEL)
        x = _ffn(x1, p["ffn_norm"], p["w_ffn_in"], p["conv_w"], p["conv_b"], p["w_ffn_out"],
                 final_norm, final=(l == DEPTH - 1))
    return x


def kernel(x_prompt, x_sample, attn_norm, w_in, a_lambda_q1, a_lambda_k1, a_lambda_q2, a_lambda_k2, a_subln, b_rpb, c_q_norm, c_kv_norm, c_w_uq, c_w_ukv, d_q_norm, d_k_norm, w_branch, w_out, ffn_norm, w_ffn_in, ffn_conv_w, ffn_conv_b, w_ffn_out, final_norm):
    tables = _rope_lane_tables()
    layers = []
    for l in range(DEPTH):
        p = _pack_layer(l, w_in, c_q_norm, c_kv_norm, c_w_uq, c_w_ukv, d_q_norm, d_k_norm)
        lam_init = 0.8 - 0.6 * math.exp(-0.3 * l)
        lam = (jnp.exp(jnp.sum(a_lambda_q1[l] * a_lambda_k1[l]))
               - jnp.exp(jnp.sum(a_lambda_q2[l] * a_lambda_k2[l])) + lam_init)
        p.update(
            attn_norm=attn_norm[l][None, :], lam=lam.reshape(1).astype(F32),
            subln=jnp.tile(a_subln[l], 2)[None, :], nb_bias=_nb_bias_tables(b_rpb[l]),
            w_branch=w_branch[l].astype(BF16), w_out=w_out[l].astype(BF16),
            ffn_norm=ffn_norm[l][None, :], w_ffn_in=w_ffn_in[l].astype(BF16),
            conv_w=ffn_conv_w[l], conv_b=ffn_conv_b[l][None, :], w_ffn_out=w_ffn_out[l].astype(BF16))
        layers.append(p)
    fn = final_norm[None, :]
    return (_trunk(x_prompt, tables, layers, fn), _trunk(x_sample, tables, layers, fn))
```

```python
import functools
import math

import numpy as np
import jax
import jax.numpy as jnp
from jax import lax
from jax.experimental import pallas as pl
from jax.experimental.pallas import tpu as pltpu

F32 = jnp.float32
BF16 = jnp.bfloat16

D_MODEL = 1024
SEQ = 2048
DEPTH = 2
GRID_W = 64
ROPE_THETA = 10000.0
EPS = 1e-6
HEADS = 4
HEAD_DIM = 64
A_D = 32
WIN_R = 8
WIN_C = 16
C_NOPE = 64
C_ROPE = 32
C_QLORA = 192
C_KVLORA = 128
D_KV_HEADS = 2
BRANCH_W = 256
FFN_DIM = 2816
LANES = 128
SUBLANES = 8
ROPE_HALF = 16
LOG2E = 1.4426950408889634

MIX_COLS = 2560
TM_IN = 1024
TM_MERGE = 1024
TQ_DIFF = 512
TILES_DIFF = 1
TQ_PLAIN = 1024
TILES_PLAIN = 2
NB_STEPS_PER_ITER = 16
NB_Q = 2 * GRID_W
NB_KEYS = 10 * GRID_W
NB_STEPS = SEQ // NB_Q
NB_VARIANTS = 5
NEG = -1e30
TM_FFN = 1024
FFN_CHUNK = 256
N_FFN_CHUNKS = FFN_DIM // FFN_CHUNK
VMEM_LIMIT = 56 * 1024 * 1024


def _rms(x, g):
    return x * lax.rsqrt(jnp.mean(x * x, axis=-1, keepdims=True) + EPS) * g


def _rope_slab(x, cos, sin_lo, sin_hi):
    return (x * cos + pltpu.roll(x, LANES - ROPE_HALF, 1) * sin_lo
            + pltpu.roll(x, ROPE_HALF, 1) * sin_hi)


def _group_mean_sq(x, width):
    sq = x * x
    hi = sq.astype(BF16)
    lo = (sq - hi.astype(F32)).astype(BF16)
    r = lax.broadcasted_iota(jnp.int32, (LANES, LANES), 0) // width
    c = lax.broadcasted_iota(jnp.int32, (LANES, LANES), 1) // width
    ones = jnp.where(r == c, 1.0, 0.0).astype(BF16)
    tot = (jnp.dot(hi, ones, preferred_element_type=F32)
           + jnp.dot(lo, ones, preferred_element_type=F32))
    return tot * (1.0 / width)


def _low_half(rows):
    return lax.broadcasted_iota(jnp.int32, (rows, LANES), 1) < HEAD_DIM


def _value_with_ones(v, shift):
    if shift:
        v = pltpu.roll(v, HEAD_DIM, 1)
    return jnp.where(_low_half(v.shape[0]), v, 1.0).astype(BF16)


def _inproj_kernel(x_ref, g_ref, w_ref, tab_ref, cqn_ref, ckvn_ref, wuq_ref, wukv_ref, dqn_ref, dkn_ref,
                   qa_ref, kat_ref, va_ref, qb_ref, kbt_ref, vb_ref,
                   qc_ref, kct_ref, vc_ref, qd_ref, kdt_ref, vd_ref):
    h = _rms(x_ref[...], g_ref[...]).astype(BF16)

    def project(first_slab, n_slabs):
        cols = slice(first_slab * LANES, (first_slab + n_slabs) * LANES)
        p = jnp.dot(h, w_ref[:, cols], preferred_element_type=F32)
        return lambda i: p[:, (i - first_slab) * LANES:(i - first_slab + 1) * LANES]

    def rope(x, variant):
        return _rope_slab(x, tab_ref[3 * variant], tab_ref[3 * variant + 1], tab_ref[3 * variant + 2])

    def store_slabs(ref, slabs):
        for i, v in enumerate(slabs):
            ref[:, i * LANES:(i + 1) * LANES] = v.astype(BF16)

    c_slab = project(16, 4)
    a_slab = project(0, 6)
    cq = jnp.concatenate([c_slab(16), c_slab(17)], axis=1)
    cq = cq * lax.rsqrt(jnp.sum(cq * cq, axis=-1, keepdims=True) * (1.0 / C_QLORA) + EPS) * cqn_ref[...]
    q = jnp.dot(cq.astype(BF16), wuq_ref[...], preferred_element_type=F32)
    ckv = _rms(c_slab(18), ckvn_ref[...])
    kv = jnp.dot(ckv.astype(BF16), wukv_ref[...], preferred_element_type=F32)
    b_slab = project(6, 6)
    store_slabs(qa_ref, [rope(a_slab(i), 0) * (A_D ** -0.5 * LOG2E) for i in range(2)])
    for i in range(2):
        kat_ref[i * LANES:(i + 1) * LANES, :] = rope(a_slab(2 + i), 0).T.astype(BF16)
    store_slabs(va_ref, [_value_with_ones(a_slab(4 + i // 2), i % 2) for i in range(HEADS)])
    d_slab = project(12, 4)
    store_slabs(qb_ref, [b_slab(6 + i) * (HEAD_DIM ** -0.5 * LOG2E) for i in range(2)])
    for i in range(2):
        kbt_ref[i * LANES:(i + 1) * LANES, :] = b_slab(8 + i).T.astype(BF16)
    store_slabs(vb_ref, [_value_with_ones(b_slab(10 + i // 2), i % 2) for i in range(HEADS)])
    kr = rope(c_slab(19), 1)
    store_slabs(qc_ref, [rope(q[:, i * LANES:(i + 1) * LANES], 1) * ((C_NOPE + C_ROPE) ** -0.5 * LOG2E)
                         for i in range(HEADS)])
    for i in range(HEADS):
        kct_ref[i * LANES:(i + 1) * LANES, :] = (kv[:, i * LANES:(i + 1) * LANES] + kr).T.astype(BF16)
    store_slabs(vc_ref, [_value_with_ones(kv[:, (HEADS + i) * LANES:(HEADS + i + 1) * LANES], 0)
                         for i in range(HEADS)])
    qd = []
    for i in range(2):
        x = d_slab(12 + i)
        x = x * lax.rsqrt(_group_mean_sq(x, HEAD_DIM) + EPS) * dqn_ref[...]
        qd.append(rope(x, 2) * (HEAD_DIM ** -0.5 * LOG2E))
    store_slabs(qd_ref, qd)
    x = d_slab(14)
    x = x * lax.rsqrt(_group_mean_sq(x, HEAD_DIM) + EPS) * dkn_ref[...]
    kdt_ref[...] = rope(x, 2).T.astype(BF16)
    store_slabs(vd_ref, [_value_with_ones(d_slab(15), i) for i in range(D_KV_HEADS)])


def _inproj(x, g, w_mix, tables, cqn, ckvn, wuq, wukv, dqn, dkn):
    b = x.shape[0]
    tm = TM_IN
    grid = (SEQ // tm, b)
    tok = lambda c: pl.BlockSpec((None, tm, c), lambda j, i: (i, j, 0))
    tr = lambda c: pl.BlockSpec((None, c, tm), lambda j, i: (i, 0, j))
    full = lambda a: pl.BlockSpec(a.shape, lambda j, i: (0,) * a.ndim)
    out_cols = [("tok", 256), ("tr", 256), ("tok", 512), ("tok", 256), ("tr", 256), ("tok", 512),
                ("tok", 512), ("tr", 512), ("tok", 512), ("tok", 256), ("tr", 128), ("tok", 256)]
    out_specs = [tok(c) if k == "tok" else tr(c) for k, c in out_cols]
    out_shape = [jax.ShapeDtypeStruct((b, SEQ, c) if k == "tok" else (b, c, SEQ), BF16) for k, c in out_cols]
    return pl.pallas_call(
        _inproj_kernel,
        grid=grid,
        in_specs=[tok(D_MODEL), full(g), full(w_mix),
                  pl.BlockSpec((9, tm, LANES), lambda j, i: (0, j, 0)),
                  full(cqn), full(ckvn), full(wuq), full(wukv), full(dqn), full(dkn)],
        out_specs=out_specs,
        out_shape=out_shape,
        compiler_params=pltpu.CompilerParams(
            dimension_semantics=("arbitrary", "arbitrary"), vmem_limit_bytes=VMEM_LIMIT),
        name="inproj",
    )(x, g, w_mix, tables, cqn, ckvn, wuq, wukv, dqn, dkn)


def _exp2_terms(s):
    return jnp.exp2(s - jnp.max(s, axis=-1, keepdims=True))


def _softmax_weights(job, s):
    return _exp2_terms(s).astype(BF16)


def _normalise_by_ones_lanes(o):
    return o * (1.0 / pltpu.roll(o, HEAD_DIM, 1))


def _join_halves(y_low, y_high):
    return jnp.where(_low_half(y_low.shape[0]), y_low, pltpu.roll(y_high, HEAD_DIM, 1))


def _pipeline(n, scores, weights, output):
    s = {0: scores(0)}
    w = {}
    for j in range(n + 1):
        if j + 1 < n:
            s[j + 1] = scores(j + 1)
        if j >= 1:
            output(j - 1, w.pop(j - 1))
        if j < n:
            w[j] = weights(j, s.pop(j))


def _tile_rows(i, t, tiles, tq):
    return pl.ds(pl.multiple_of((i * tiles + t) * tq, tq), tq)


def _attn_a_kernel(lam_ref, q_ref, kt_ref, v_ref, g_ref, o_ref, *, out_scale):
    lam = lam_ref[0]
    per_tile = 2 * HEADS

    def tiles(i, carry):
        def scores(job):
            t, j = divmod(job, per_tile)
            return jnp.dot(q_ref[_tile_rows(i, t, TILES_DIFF, TQ_DIFF), A_D * j:A_D * (j + 1)],
                           kt_ref[A_D * j:A_D * (j + 1), :], preferred_element_type=F32)

        held = {}

        def output(job, p):
            t, j = divmod(job, per_tile)
            head, c = divmod(j, 2)
            pair, half = divmod(head, 2)
            y = _normalise_by_ones_lanes(
                jnp.dot(p, v_ref[:, head * LANES:(head + 1) * LANES], preferred_element_type=F32))
            if c == 0:
                held["first"] = y
                return
            d = held.pop("first") - lam * y
            ms = jnp.sum(jnp.where(_low_half(TQ_DIFF), d * d, 0.0), axis=-1, keepdims=True) * (1.0 / HEAD_DIM)
            yn = d * lax.rsqrt(ms + EPS)
            if half == 0:
                held["low"] = yn
                return
            o_ref[_tile_rows(i, t, TILES_DIFF, TQ_DIFF), pair * LANES:(pair + 1) * LANES] = (
                _join_halves(held.pop("low"), yn) * g_ref[...] * out_scale).astype(BF16)

        _pipeline(TILES_DIFF * per_tile, scores, _softmax_weights, output)
        return carry

    lax.fori_loop(0, SEQ // (TQ_DIFF * TILES_DIFF), tiles, 0)


def _attn_c_kernel(q_ref, kt_ref, v_ref, o_ref):
    def tiles(i, carry):
        def scores(job):
            t, head = divmod(job, HEADS)
            cols = slice(head * LANES, (head + 1) * LANES)
            return jnp.dot(q_ref[_tile_rows(i, t, TILES_PLAIN, TQ_PLAIN), cols], kt_ref[cols, :], preferred_element_type=F32)

        held = {}

        def output(job, p):
            t, head = divmod(job, HEADS)
            pair, half = divmod(head, 2)
            y = _normalise_by_ones_lanes(
                jnp.dot(p, v_ref[:, head * LANES:(head + 1) * LANES], preferred_element_type=F32))
            if half == 0:
                held["low"] = y
                return
            o_ref[_tile_rows(i, t, TILES_PLAIN, TQ_PLAIN), pair * LANES:(pair + 1) * LANES] = _join_halves(held.pop("low"), y).astype(BF16)

        _pipeline(TILES_PLAIN * HEADS, scores, _softmax_weights, output)
        return carry

    lax.fori_loop(0, SEQ // (TQ_PLAIN * TILES_PLAIN), tiles, 0)


def _attn_d_kernel(q_ref, kt_ref, v_ref, o_ref):
    def tiles(i, carry):
        def scores(job):
            t, head = divmod(job, HEADS)
            kv_head = head // 2
            return jnp.dot(q_ref[_tile_rows(i, t, TILES_PLAIN, TQ_PLAIN), head * HEAD_DIM:(head + 1) * HEAD_DIM],
                           kt_ref[kv_head * HEAD_DIM:(kv_head + 1) * HEAD_DIM, :], preferred_element_type=F32)

        held = {}

        def output(job, p):
            t, head = divmod(job, HEADS)
            kv_head, g = divmod(head, 2)
            y = _normalise_by_ones_lanes(
                jnp.dot(p, v_ref[:, kv_head * LANES:(kv_head + 1) * LANES], preferred_element_type=F32))
            if g == 0:
                held["low"] = y
                return
            o_ref[_tile_rows(i, t, TILES_PLAIN, TQ_PLAIN), kv_head * LANES:(kv_head + 1) * LANES] = (
                _join_halves(held.pop("low"), y).astype(BF16))

        _pipeline(TILES_PLAIN * HEADS, scores, _softmax_weights, output)
        return carry

    lax.fori_loop(0, SEQ // (TQ_PLAIN * TILES_PLAIN), tiles, 0)


def _attn_full(kernel, q, kt, v, extra_in=(), extra_specs=(), scalar=None, name=None):
    b = q.shape[0]
    per_batch = lambda a: pl.BlockSpec((None,) + a.shape[1:], lambda i: (i, 0, 0))
    in_specs = [per_batch(q), per_batch(kt), per_batch(v)]
    args = [q, kt, v]
    if scalar is not None:
        in_specs = [pl.BlockSpec(memory_space=pltpu.SMEM)] + in_specs
        args = [scalar] + args
    in_specs += list(extra_specs)
    args += list(extra_in)
    return pl.pallas_call(
        kernel,
        grid=(b,),
        in_specs=in_specs,
        out_specs=pl.BlockSpec((None, SEQ, BRANCH_W), lambda i: (i, 0, 0)),
        out_shape=jax.ShapeDtypeStruct((b, SEQ, BRANCH_W), BF16),
        compiler_params=pltpu.CompilerParams(
            dimension_semantics=("arbitrary",), vmem_limit_bytes=VMEM_LIMIT),
        name=name,
    )(*args)


def _attn_b_kernel(q_ref, kt_ref, v_ref, bias_ref, o_ref):
    def steps(i, carry):
        def window(t):
            st = i * NB_STEPS_PER_ITER + t
            start = pl.multiple_of(jnp.clip(st - 2, 0, NB_STEPS - 5) * NB_Q, NB_Q)
            variant = jnp.minimum(st, 2) + jnp.maximum(st - (NB_STEPS - 3), 0)
            return pl.ds(pl.multiple_of(st * NB_Q, NB_Q), NB_Q), pl.ds(start, NB_KEYS), variant

        def scores(job):
            t, head = divmod(job, HEADS)
            rows, keys, variant = window(t)
            cols = slice(head * HEAD_DIM, (head + 1) * HEAD_DIM)
            s = jnp.dot(q_ref[rows, cols], kt_ref[cols, keys], preferred_element_type=F32)
            return s + bias_ref[variant, head]

        held = {}

        def output(job, p):
            t, head = divmod(job, HEADS)
            rows, keys, _ = window(t)
            pair, half = divmod(head, 2)
            y = _normalise_by_ones_lanes(
                jnp.dot(p, v_ref[keys, head * LANES:(head + 1) * LANES], preferred_element_type=F32))
            if half == 0:
                held["low"] = y
                return
            o_ref[rows, pair * LANES:(pair + 1) * LANES] = _join_halves(held.pop("low"), y).astype(BF16)

        _pipeline(NB_STEPS_PER_ITER * HEADS, scores, _softmax_weights, output)
        return carry

    lax.fori_loop(0, NB_STEPS // NB_STEPS_PER_ITER, steps, 0)


def _attn_b(q, k, v, bias):
    b = q.shape[0]
    per_batch = lambda a: pl.BlockSpec((None,) + a.shape[1:], lambda i: (i, 0, 0))
    return pl.pallas_call(
        _attn_b_kernel,
        grid=(b,),
        in_specs=[per_batch(q), per_batch(k), per_batch(v),
                  pl.BlockSpec(bias.shape, lambda i: (0, 0, 0, 0), pipeline_mode=pl.Buffered(1))],
        out_specs=pl.BlockSpec((None, SEQ, BRANCH_W), lambda i: (i, 0, 0)),
        out_shape=jax.ShapeDtypeStruct((b, SEQ, BRANCH_W), BF16),
        compiler_params=pltpu.CompilerParams(
            dimension_semantics=("arbitrary",), vmem_limit_bytes=VMEM_LIMIT),
        name="attn_b",
    )(q, k, v, bias)


def _nb_bias_tables(rpb):
    rows = SEQ // GRID_W
    steps = np.array([0, 1, 2, NB_STEPS - 2, NB_STEPS - 1])
    start_row = np.clip(steps - 2, 0, NB_STEPS - 5) * 2
    qr = np.arange(2)
    kr = np.arange(NB_KEYS // GRID_W)
    r = 2 * steps[:, None, None] + qr[None, :, None]
    krow = start_row[:, None, None] + kr[None, None, :]
    rs = np.clip(r - WIN_R // 2, 0, rows - WIN_R)
    row_ok = (krow >= rs) & (krow < rs + WIN_R)
    dr = np.clip(krow - r + WIN_R - 1, 0, 2 * WIN_R - 2)
    qc = np.arange(GRID_W)[:, None]
    kc = np.arange(GRID_W)[None, :]
    cs = np.clip(qc - WIN_C // 2, 0, GRID_W - WIN_C)
    col_ok = (kc >= cs) & (kc < cs + WIN_C)
    dc = np.clip(kc - qc + WIN_C - 1, 0, 2 * WIN_C - 2)
    row_sel = np.eye(2 * WIN_R - 1, dtype=np.float32)[dr.reshape(-1)]
    col_sel = np.eye(2 * WIN_C - 1, dtype=np.float32)[dc.reshape(-1)].T
    hp = lax.Precision.HIGHEST
    t = jnp.einsum("ar,hrc->hac", row_sel, rpb.astype(F32), precision=hp)
    t = jnp.einsum("hac,ck->hak", t, col_sel, precision=hp)
    nkr = NB_KEYS // GRID_W
    t = t.reshape(HEADS, NB_VARIANTS, 2, nkr, GRID_W, GRID_W)
    t = jnp.transpose(t, (1, 0, 2, 4, 3, 5)).reshape(NB_VARIANTS, HEADS, NB_Q, NB_KEYS)
    valid = row_ok[:, :, None, :, None] & col_ok[None, None, :, None, :]
    valid = valid.reshape(NB_VARIANTS, 1, NB_Q, NB_KEYS)
    return jnp.where(jnp.asarray(valid), t * LOG2E, NEG)


def _merge_kernel(x_ref, g_ref, oa_ref, ob_ref, oc_ref, od_ref, wg_ref, wb_ref, wo_ref, y_ref):
    x = x_ref[...]
    h = _rms(x, g_ref[...]).astype(BF16)
    merged = None
    for i, o_ref in enumerate((oa_ref, ob_ref, oc_ref, od_ref)):
        gate = jax.nn.sigmoid(jnp.dot(h, wg_ref[:, i * D_MODEL:(i + 1) * D_MODEL], preferred_element_type=F32))
        term = gate * jnp.dot(o_ref[...], wb_ref[i], preferred_element_type=F32)
        merged = term if merged is None else merged + term
    y_ref[...] = x + jnp.dot(merged.astype(BF16), wo_ref[...], preferred_element_type=F32)


def _merge(x, g, oa, ob, oc, od, wg, wb, wo):
    t = x.shape[0]
    tm = TM_MERGE
    tok = lambda c: pl.BlockSpec((tm, c), lambda i: (i, 0))
    full = lambda a: pl.BlockSpec(a.shape, lambda i: (0,) * a.ndim, pipeline_mode=pl.Buffered(1))
    return pl.pallas_call(
        _merge_kernel,
        grid=(t // tm,),
        in_specs=[tok(D_MODEL), full(g), tok(BRANCH_W), tok(BRANCH_W), tok(BRANCH_W), tok(BRANCH_W),
                  full(wg), full(wb), full(wo)],
        out_specs=tok(D_MODEL),
        out_shape=jax.ShapeDtypeStruct((t, D_MODEL), F32),
        compiler_params=pltpu.CompilerParams(
            dimension_semantics=("arbitrary",), vmem_limit_bytes=VMEM_LIMIT),
        name="merge",
    )(x, g, oa, ob, oc, od, wg, wb, wo)


def _ffn_kernel(xp_ref, x_ref, xn_ref, g_ref, win_ref, cw_ref, cb_ref, wo_ref, fg_ref, y_ref, act_scr, *, final):
    j = pl.program_id(1)
    tm = x_ref.shape[0]
    ext = tm + 2 * SUBLANES
    x = x_ref[...]
    h = _rms(jnp.concatenate([xp_ref[...], x, xn_ref[...]], axis=0), g_ref[...]).astype(BF16)
    keep_prev = (j > 0).astype(F32)
    keep_next = (j < pl.num_programs(1) - 1).astype(F32)
    row = lax.broadcasted_iota(jnp.int32, (ext, 1), 0)
    halo_scale = jnp.where(row < SUBLANES, keep_prev, jnp.where(row >= tm + SUBLANES, keep_next, 1.0))
    for c in range(N_FFN_CHUNKS):
        cols = slice(c * FFN_CHUNK, (c + 1) * FFN_CHUNK)
        u = jnp.dot(h, win_ref[:, cols], preferred_element_type=F32)
        g = jnp.dot(h, win_ref[:, FFN_DIM + c * FFN_CHUNK:FFN_DIM + (c + 1) * FFN_CHUNK],
                    preferred_element_type=F32) * halo_scale
        gc = (pltpu.roll(g, 1, 0) * cw_ref[0:1, cols] + g * cw_ref[1:2, cols]
              + pltpu.roll(g, ext - 1, 0) * cw_ref[2:3, cols] + cb_ref[:, cols])
        act = jax.nn.silu(gc) * u
        act_scr[:, cols] = act[SUBLANES:SUBLANES + tm].astype(BF16)
    y = x + jnp.dot(act_scr[...], wo_ref[...], preferred_element_type=F32)
    if final:
        y = _rms(y, fg_ref[...])
    y_ref[...] = y


def _ffn(x, g, w_in, conv_w, conv_b, w_out, final_g, final):
    b = x.shape[0]
    tm = TM_FFN
    per = tm // SUBLANES
    last = SEQ // SUBLANES - 1
    const = lambda a: pl.BlockSpec(a.shape, lambda i, j: (0,) * a.ndim, pipeline_mode=pl.Buffered(1))
    return pl.pallas_call(
        functools.partial(_ffn_kernel, final=final),
        grid=(b, SEQ // tm),
        in_specs=[pl.BlockSpec((None, SUBLANES, D_MODEL), lambda i, j: (i, jnp.maximum(j * per - 1, 0), 0)),
                  pl.BlockSpec((None, tm, D_MODEL), lambda i, j: (i, j, 0)),
                  pl.BlockSpec((None, SUBLANES, D_MODEL), lambda i, j: (i, jnp.minimum((j + 1) * per, last), 0)),
                  const(g), const(w_in), const(conv_w), const(conv_b), const(w_out), const(final_g)],
        out_specs=pl.BlockSpec((None, tm, D_MODEL), lambda i, j: (i, j, 0)),
        out_shape=jax.ShapeDtypeStruct((b, SEQ, D_MODEL), F32),
        scratch_shapes=[pltpu.VMEM((tm, FFN_DIM), BF16)],
        compiler_params=pltpu.CompilerParams(
            dimension_semantics=("arbitrary", "arbitrary"), vmem_limit_bytes=VMEM_LIMIT),
        name="ffn",
    )(x, x, x, g, w_in, conv_w, conv_b, w_out, final_g)


def _rope_lane_tables():
    t = jnp.arange(SEQ)
    inv_freq = ROPE_THETA ** (-jnp.arange(0, 2 * ROPE_HALF, 2, dtype=F32) / (2 * ROPE_HALF))
    ang = lambda pos: pos.astype(F32)[:, None] * inv_freq[None, :]
    lane = np.arange(LANES)
    freq = lane % ROPE_HALF
    first = jnp.asarray((lane % (2 * ROPE_HALF)) < ROPE_HALF)[None, :]
    ang_seq = ang(t)[:, freq]
    ang_axial = jnp.where(jnp.asarray((lane // (2 * ROPE_HALF)) % 2 == 0)[None, :],
                          ang(t // GRID_W)[:, freq], ang(t % GRID_W)[:, freq])
    active = jnp.asarray((lane >= C_NOPE) & (lane < C_NOPE + C_ROPE))[None, :]

    def trio(a, on):
        cos, sin = jnp.cos(a), jnp.sin(a)
        return [jnp.where(on, cos, 1.0), jnp.where(on & first, -sin, 0.0), jnp.where(on & ~first, sin, 0.0)]

    everywhere = jnp.ones((1, LANES), bool)
    return jnp.stack(trio(ang_seq, everywhere) + trio(ang_seq, active) + trio(ang_axial, everywhere))


def _pack_layer(l, w_in, c_q_norm, c_kv_norm, c_w_uq, c_w_ukv, d_q_norm, d_k_norm):
    w = w_in[l]
    a, bb, cc, dd, gate = jnp.split(w, [768, 1536, 1888, 2400], axis=1)
    z = lambda n: jnp.zeros((D_MODEL, n), F32)
    cq, ckv, krope = cc[:, :C_QLORA], cc[:, C_QLORA:C_QLORA + C_KVLORA], cc[:, C_QLORA + C_KVLORA:]
    w_mix = jnp.concatenate([a, bb, dd, cq, z(64), ckv, z(C_NOPE), krope, z(LANES - C_NOPE - C_ROPE)], axis=1)
    assert w_mix.shape[1] == MIX_COLS
    wuq = c_w_uq[l].reshape(C_QLORA, HEADS, C_NOPE + C_ROPE)
    wuq = jnp.pad(wuq, ((0, 256 - C_QLORA), (0, 0), (0, LANES - C_NOPE - C_ROPE))).reshape(256, HEADS * LANES)
    wukv = c_w_ukv[l].reshape(C_KVLORA, HEADS, C_NOPE + HEAD_DIM)
    halves = ((0, 0), (0, 0), (0, LANES - HEAD_DIM))
    wk = jnp.pad(wukv[:, :, :C_NOPE], halves).reshape(C_KVLORA, HEADS * LANES)
    wv = jnp.pad(wukv[:, :, C_NOPE:], halves).reshape(C_KVLORA, HEADS * LANES)
    wukv = jnp.concatenate([wk, wv], axis=1)
    cqn = jnp.pad(c_q_norm[l], (0, 256 - C_QLORA))[None, :]
    return dict(
        w_mix=w_mix.astype(BF16), w_gate=gate.astype(BF16), wuq=wuq.astype(BF16), wukv=wukv.astype(BF16),
        cqn=cqn, ckvn=c_kv_norm[l][None, :],
        dqn=jnp.tile(d_q_norm[l], 2)[None, :], dkn=jnp.tile(d_k_norm[l], 2)[None, :])


def _trunk(x, tables, layers, final_norm):
    b = x.shape[0]
    for l, p in enumerate(layers):
        (qa, kat, va, qb, kb, vb, qc, kct, vc, qd, kdt, vd) = _inproj(
            x, p["attn_norm"], p["w_mix"], tables, p["cqn"], p["ckvn"], p["wuq"], p["wukv"], p["dqn"], p["dkn"])
        lam_init = 0.8 - 0.6 * math.exp(-0.3 * l)
        oa = _attn_full(functools.partial(_attn_a_kernel, out_scale=1.0 - lam_init), qa, kat, va,
                        extra_in=(p["subln"],), extra_specs=(pl.BlockSpec((1, LANES), lambda i: (0, 0)),),
                        scalar=p["lam"], name="attn_a")
        ob = _attn_b(qb, kb, vb, p["nb_bias"])
        oc = _attn_full(_attn_c_kernel, qc, kct, vc, name="attn_c")
        od = _attn_full(_attn_d_kernel, qd, kdt, vd, name="attn_d")
        flat = lambda a: a.reshape(b * SEQ, a.shape[-1])
        x1 = _merge(flat(x), p["attn_norm"], flat(oa), flat(ob), flat(oc), flat(od),
                    p["w_gate"], p["w_branch"], p["w_out"]).reshape(b, SEQ, D_MODEL)
        x = _ffn(x1, p["ffn_norm"], p["w_ffn_in"], p["conv_w"], p["conv_b"], p["w_ffn_out"],
                 final_norm, final=(l == DEPTH - 1))
    return x


def kernel(x_prompt, x_sample, attn_norm, w_in, a_lambda_q1, a_lambda_k1, a_lambda_q2, a_lambda_k2, a_subln, b_rpb, c_q_norm, c_kv_norm, c_w_uq, c_w_ukv, d_q_norm, d_k_norm, w_branch, w_out, ffn_norm, w_ffn_in, ffn_conv_w, ffn_conv_b, w_ffn_out, final_norm):
    tables = _rope_lane_tables()
    layers = []
    for l in range(DEPTH):
        p = _pack_layer(l, w_in, c_q_norm, c_kv_norm, c_w_uq, c_w_ukv, d_q_norm, d_k_norm)
        lam_init = 0.8 - 0.6 * math.exp(-0.3 * l)
        lam = (jnp.exp(jnp.sum(a_lambda_q1[l] * a_lambda_k1[l]))
               - jnp.exp(jnp.sum(a_lambda_q2[l] * a_lambda_k2[l])) + lam_init)
        p.update(
            attn_norm=attn_norm[l][None, :], lam=lam.reshape(1).astype(F32),
            subln=jnp.tile(a_subln[l], 2)[None, :], nb_bias=_nb_bias_tables(b_rpb[l]),
            w_branch=w_branch[l].astype(BF16), w_out=w_out[l].astype(BF16),
            ffn_norm=ffn_norm[l][None, :], w_ffn_in=w_ffn_in[l].astype(BF16),
            conv_w=ffn_conv_w[l], conv_b=ffn_conv_b[l][None, :], w_ffn_out=w_ffn_out[l].astype(BF16))
        layers.append(p)
    fn = final_norm[None, :]
    return (_trunk(x_prompt, tables, layers, fn), _trunk(x_sample, tables, layers, fn))
```
